```python
import functools
import math
import jax
import jax.numpy as jnp
from jax import lax
import numpy as np

D_MODEL = 1024
BATCH = 2
SEQ = 8192
DEPTH = 4

GRID_W = 64
CTX_LEN = 256
N_EVEN = (DEPTH + 1) // 2
N_ODD = DEPTH // 2
EPS = 1e-6
ROPE_BASE = 10000.0
F32 = jnp.float32

S5_WIDTH = D_MODEL // 2
S5_GROUP = 16
S5_GROUPS = S5_WIDTH // S5_GROUP
S5_STATE = 64

MLA_HEADS = 8
MLA_NOPE = 64
MLA_ROPE = 32
MLA_V = 64
MLA_Q_RANK = D_MODEL // 4
MLA_KV_RANK = D_MODEL // 8
ATT_BLOCK = 128

EVEN_SIZES = (S5_WIDTH, MLA_Q_RANK, MLA_KV_RANK, MLA_ROPE)
EVEN_IN = sum(EVEN_SIZES)
EVEN_MIX = S5_WIDTH + MLA_HEADS * MLA_V

RET_HEADS = 4
RET_DK = 128
RET_DV = 128
RET_CHUNK = 128
RET_QK = RET_HEADS * RET_DK
RET_VW = RET_HEADS * RET_DV
RET_DECAY_OFFSETS = (0.0, 0.5)

HG_HEADS = 4
HG_DK = 128
HG_DV = 128
HG_CHUNK = 64
HG_QK = HG_HEADS * HG_DK
HG_VW = HG_HEADS * HG_DV

ODD_SIZES = (RET_QK, RET_QK, RET_VW, RET_VW, HG_QK, HG_QK, HG_QK, HG_VW, HG_VW)
ODD_IN = sum(ODD_SIZES)
ODD_MIX = RET_VW + HG_VW

D_FF = -(-8 * D_MODEL // (3 * 256)) * 256

kernel_name = 'hybrid_s5_mla_retention_hgrn2_diffusion_trunk'


def rms_norm(x, g):
    xf = x.astype(F32)
    y = xf * lax.rsqrt(jnp.mean(xf * xf, axis=-1, keepdims=True) + EPS)
    return (y * g.astype(F32)).astype(x.dtype)


def split_heads(t, n_heads):
    b, l, w = t.shape
    return t.reshape(b, l, n_heads, w // n_heads).transpose(0, 2, 1, 3)


def merge_heads(t):
    b, h, l, d = t.shape
    return t.transpose(0, 2, 1, 3).reshape(b, l, h * d)


def head_norm(o, g, center):
    of = o.astype(F32)
    if center:
        of = of - jnp.mean(of, axis=-1, keepdims=True)
    of = of * lax.rsqrt(jnp.mean(of * of, axis=-1, keepdims=True) + EPS)
    return (merge_heads(of) * g.astype(F32)).astype(o.dtype)


def split_cols(t, sizes):
    idx = []
    s = 0
    for n in sizes[:-1]:
        s += n
        idx.append(s)
    return jnp.split(t, idx, axis=-1)


def axial_rope(rows, dim):
    r, col = jnp.meshgrid(jnp.arange(rows, dtype=F32), jnp.arange(GRID_W, dtype=F32), indexing='ij')
    quarter = dim // 4
    inv = ROPE_BASE ** (-jnp.arange(quarter, dtype=F32) / quarter)
    ang = jnp.concatenate([r.reshape(-1, 1) * inv, col.reshape(-1, 1) * inv], axis=-1)
    return jnp.cos(ang), jnp.sin(ang)


def apply_rope(x, cos, sin):
    xp = x.reshape(x.shape[:-1] + (x.shape[-1] // 2, 2))
    x1, x2 = xp[..., 0], xp[..., 1]
    cos = cos.astype(x.dtype)
    sin = sin.astype(x.dtype)
    return jnp.stack([x1 * cos - x2 * sin, x1 * sin + x2 * cos], axis=-1).reshape(x.shape)


def swiglu(h, w1, w3, w2):
    return (jax.nn.silu(h @ w1) * (h @ w3)) @ w2


def context_then_latent(dir_fn, ctx_in, lat_in, s0, axis, reverse):
    flip = (lambda t: jnp.flip(t, axis)) if reverse else (lambda t: t)
    o_c, s_c = dir_fn(*[flip(t) for t in ctx_in], s0)
    o_x, _ = dir_fn(*[flip(t) for t in lat_in], s_c)
    return flip(o_x), flip(o_c)


def block_attention(q, k, v):
    b, h, lq, d = q.shape
    nb = lq // ATT_BLOCK
    qb = q.reshape(b, h, nb, ATT_BLOCK, d).transpose(2, 0, 1, 3, 4)
    scale = d ** -0.5

    def attend(qblk):
        s = jnp.einsum('bhqd,bhkd->bhqk', qblk, k).astype(F32) * scale
        p = jax.nn.softmax(s, axis=-1).astype(v.dtype)
        return jnp.einsum('bhqk,bhkd->bhqd', p, v)

    o = lax.map(attend, qb)
    return o.transpose(1, 2, 0, 3, 4).reshape(b, h, lq, v.shape[-1])


def s5_discretize(a_re, a_im, log_dt, b_re, b_im):
    dt = jnp.exp(log_dt)[:, None]
    mag = jnp.exp(a_re * dt)
    ab_re = mag * jnp.cos(a_im * dt)
    ab_im = mag * jnp.sin(a_im * dt)
    nr, ni = ab_re - 1.0, ab_im
    den = a_re * a_re + a_im * a_im
    fr = (nr * a_re + ni * a_im) / den
    fi = (ni * a_re - nr * a_im) / den
    bb_re = fr[..., None] * b_re - fi[..., None] * b_im
    bb_im = fr[..., None] * b_im + fi[..., None] * b_re
    return ab_re, ab_im, bb_re, bb_im


def _complex_affine_combine(e1, e2):
    a1r, a1i, b1r, b1i = e1
    a2r, a2i, b2r, b2i = e2
    return (a2r * a1r - a2i * a1i, a2r * a1i + a2i * a1r,
            a2r * b1r - a2i * b1i + b2r, a2r * b1i + a2i * b1r + b2i)


def s5_direction(u, s0, ab_re, ab_im, bb_re, bb_im, c_re, c_im):
    bu_re = jnp.einsum('gpk,blgk->blgp', bb_re, u)
    bu_im = jnp.einsum('gpk,blgk->blgp', bb_im, u)
    h0r, h0i = s0
    bu_re = bu_re.at[:, 0].add(ab_re * h0r - ab_im * h0i)
    bu_im = bu_im.at[:, 0].add(ab_re * h0i + ab_im * h0r)
    ar = jnp.broadcast_to(ab_re.astype(bu_re.dtype), bu_re.shape)
    ai = jnp.broadcast_to(ab_im.astype(bu_re.dtype), bu_re.shape)
    _, _, hr, hi = lax.associative_scan(_complex_affine_combine, (ar, ai, bu_re, bu_im), axis=1)
    y = jnp.einsum('gkp,blgp->blgk', c_re, hr) - jnp.einsum('gkp,blgp->blgk', c_im, hi)
    return y, (hr[:, -1], hi[:, -1])


def s5_mixer(ux, uc, a_re, a_im, log_dt, b_re, b_im, c_re, c_im, d, w_glu, ctx_out):
    b, lx, _ = ux.shape
    lc = uc.shape[1]
    gx = ux.reshape(b, lx, S5_GROUPS, S5_GROUP)
    gc = uc.reshape(b, lc, S5_GROUPS, S5_GROUP)
    zero = jnp.zeros((b, S5_GROUPS, S5_STATE), ux.dtype)
    yx = d * ux
    yc = d * uc
    for r in range(2):
        ab_re, ab_im, bb_re, bb_im = s5_discretize(a_re[r], a_im[r], log_dt[r], b_re[r], b_im[r])
        fn = functools.partial(s5_direction, ab_re=ab_re, ab_im=ab_im, bb_re=bb_re, bb_im=bb_im,
                               c_re=c_re[r], c_im=c_im[r])
        ox, oc = context_then_latent(fn, (gc,), (gx,), (zero, zero), 1, r == 1)
        yx = yx + ox.reshape(b, lx, S5_WIDTH)
        yc = yc + oc.reshape(b, lc, S5_WIDTH)

    def glu(y):
        z = jax.nn.gelu(y)
        return z * jax.nn.sigmoid(z @ w_glu)

    return glu(yx), (glu(yc) if ctx_out else None)


def mla_queries(cq, q_norm, w_uq, rope):
    b, l, _ = cq.shape
    q = (rms_norm(cq, q_norm) @ w_uq).reshape(b, l, MLA_HEADS, MLA_NOPE + MLA_ROPE).transpose(0, 2, 1, 3)
    if rope is not None:
        q = jnp.concatenate([q[..., :MLA_NOPE], apply_rope(q[..., MLA_NOPE:], *rope)], axis=-1)
    return q


def mla_keys_values(ckv, kr, kv_norm, w_ukv, rope):
    b, l, _ = ckv.shape
    kv = (rms_norm(ckv, kv_norm) @ w_ukv).reshape(b, l, MLA_HEADS, MLA_NOPE + MLA_V).transpose(0, 2, 1, 3)
    kr = kr[:, None]
    if rope is not None:
        kr = apply_rope(kr, *rope)
    k = jnp.concatenate([kv[..., :MLA_NOPE], jnp.broadcast_to(kr, (b, MLA_HEADS, l, MLA_ROPE))], axis=-1)
    return k, kv[..., MLA_NOPE:]


def even_mixer(hx, hc, w_in, a_re, a_im, log_dt, b_re, b_im, c_re, c_im, d, w_glu,
               q_norm, w_uq, kv_norm, w_ukv, rope, ctx_out):
    ux, cqx, ckvx, krx = split_cols(hx @ w_in, EVEN_SIZES)
    uc, cqc, ckvc, krc = split_cols(hc @ w_in, EVEN_SIZES)
    yx, yc = s5_mixer(ux, uc, a_re, a_im, log_dt, b_re, b_im, c_re, c_im, d, w_glu, ctx_out)
    kx, vx = mla_keys_values(ckvx, krx, kv_norm, w_ukv, rope)
    kc, vc = mla_keys_values(ckvc, krc, kv_norm, w_ukv, None)
    qx = mla_queries(cqx, q_norm, w_uq, rope)
    ox = block_attention(qx, jnp.concatenate([kc, kx], axis=2), jnp.concatenate([vc, vx], axis=2))
    out_x = jnp.concatenate([yx, merge_heads(ox)], axis=-1)
    if not ctx_out:
        return out_x, None
    oc = block_attention(mla_queries(cqc, q_norm, w_uq, None), kc, vc)
    return out_x, jnp.concatenate([yc, merge_heads(oc)], axis=-1)


def retention_direction(q, k, v, s0, log_gamma):
    b, h, l, dk = q.shape
    dv = v.shape[-1]
    c = RET_CHUNK
    n = l // c
    dt = q.dtype
    pos = jnp.arange(c, dtype=F32)
    lg = log_gamma[:, None]
    rel = pos[:, None] - pos[None, :]
    dmask = jnp.where(rel >= 0, jnp.exp(lg[:, :, None] * jnp.maximum(rel, 0.0)), 0.0).astype(dt)
    k_w = jnp.exp(lg * (c - 1 - pos)).astype(dt)
    q_w = jnp.exp(lg * (pos + 1)).astype(dt)
    g_c = jnp.exp(log_gamma * c).astype(dt)[None, :, None, None]
    qc = q.reshape(b, h, n, c, dk)
    kc = k.reshape(b, h, n, c, dk)
    vc = v.reshape(b, h, n, c, dv)
    inner = jnp.einsum('bhntd,bhnsd->bhnts', qc, kc) * dmask[:, None]
    o_intra = jnp.einsum('bhnts,bhnse->bhnte', inner, vc)
    chunk_kv = jnp.einsum('bhnsd,hs,bhnse->nbhde', kc, k_w, vc)

    def step(s, kv):
        return (g_c * s + kv).astype(s.dtype), s

    s_fin, s_prev = lax.scan(step, s0, chunk_kv)
    o_cross = jnp.einsum('bhntd,ht,nbhde->bhnte', qc, q_w, s_prev)
    return (o_intra + o_cross).reshape(b, h, l, dv), s_fin


def retention_mixer(parts_x, parts_c, gn, rope, ctx_out):
    q_x, k_x, v_x, g_x = parts_x
    q_c, k_c, v_c, g_c = parts_c
    scale = RET_DK ** -0.5
    qxh = apply_rope(split_heads(q_x, RET_HEADS), *rope)
    kxh = apply_rope(split_heads(k_x, RET_HEADS), *rope) * scale
    vxh = split_heads(v_x, RET_HEADS)
    qch = split_heads(q_c, RET_HEADS)
    kch = split_heads(k_c, RET_HEADS) * scale
    vch = split_heads(v_c, RET_HEADS)
    s0 = jnp.zeros((q_x.shape[0], RET_HEADS, RET_DK, RET_DV), q_x.dtype)
    ox = jnp.zeros_like(vxh)
    oc = jnp.zeros_like(vch)
    for r, offset in enumerate(RET_DECAY_OFFSETS):
        log_gamma = jnp.log1p(-jnp.exp2(-(5.0 + offset) - jnp.arange(RET_HEADS, dtype=F32)))
        fn = functools.partial(retention_direction, log_gamma=log_gamma)
        o_x, o_c = context_then_latent(fn, (qch, kch, vch), (qxh, kxh, vxh), s0, 2, r == 1)
        ox = ox + o_x
        oc = oc + o_c
    out_x = head_norm(ox, gn, True) * jax.nn.silu(g_x)
    return out_x, (head_norm(oc, gn, True) * jax.nn.silu(g_c) if ctx_out else None)


def hgrn2_direction(q, k, v, logf, s0):
    b, h, l, dk = q.shape
    dv = v.shape[-1]
    n = l // HG_CHUNK
    causal = jnp.tril(jnp.ones((HG_CHUNK, HG_CHUNK), bool))[:, :, None]

    def chunks(t):
        return t.reshape(b, h, n, HG_CHUNK, t.shape[-1]).transpose(2, 0, 1, 3, 4)

    def step(s, blk):
        qb, kb, vb, lf = blk
        cum = jnp.cumsum(lf.astype(F32), axis=2)
        pair = jnp.where(causal, jnp.exp(jnp.minimum(cum[:, :, :, None] - cum[:, :, None], 0.0)), 0.0)
        att = jnp.einsum('bhtd,bhsd,bhtsd->bhts', qb, kb, pair.astype(qb.dtype))
        o = att @ vb + jnp.einsum('bhtd,bhde->bhte', qb * jnp.exp(cum).astype(qb.dtype), s)
        end = cum[:, :, -1:]
        s_new = (jnp.exp(end[:, :, 0])[..., None].astype(s.dtype) * s
                 + jnp.einsum('bhsd,bhse->bhde', kb * jnp.exp(end - cum).astype(kb.dtype), vb))
        return s_new.astype(s.dtype), o

    s_fin, o = lax.scan(step, s0, (chunks(q), chunks(k), chunks(v), chunks(logf)))
    return o.transpose(1, 2, 0, 3, 4).reshape(b, h, l, dv), s_fin


def hgrn2_mixer(parts_x, parts_c, lb, gn, ctx_out):
    lbh = lb.reshape(HG_HEADS, 1, HG_DK)

    def prep(q, ff, fb, i):
        gates = []
        for fpre in (ff, fb):
            f = lbh + (1.0 - lbh) * jax.nn.sigmoid(split_heads(fpre, HG_HEADS).astype(F32))
            gates.append(((1.0 - f).astype(q.dtype), jnp.log(f).astype(q.dtype)))
        return split_heads(q, HG_HEADS), split_heads(i, HG_HEADS), gates

    q_x, ff_x, fb_x, i_x, g_x = parts_x
    q_c, ff_c, fb_c, i_c, g_c = parts_c
    qxh, ixh, gates_x = prep(q_x, ff_x, fb_x, i_x)
    qch, ich, gates_c = prep(q_c, ff_c, fb_c, i_c)
    s0 = jnp.zeros((q_x.shape[0], HG_HEADS, HG_DK, HG_DV), q_x.dtype)
    ox = jnp.zeros_like(ixh)
    oc = jnp.zeros_like(ich)
    for r in range(2):
        o_x, o_c = context_then_latent(hgrn2_direction,
                                       (qch, gates_c[r][0], ich, gates_c[r][1]),
                                       (qxh, gates_x[r][0], ixh, gates_x[r][1]), s0, 2, r == 1)
        ox = ox + o_x
        oc = oc + o_c
    out_x = head_norm(ox, gn, False) * jax.nn.silu(g_x)
    return out_x, (head_norm(oc, gn, False) * jax.nn.silu(g_c) if ctx_out else None)


def odd_mixer(hx, hc, w_in, ret_gn, lb, hg_gn, rope, ctx_out):
    px = split_cols(hx @ w_in, ODD_SIZES)
    pc = split_cols(hc @ w_in, ODD_SIZES)
    rx, rc = retention_mixer(px[:4], pc[:4], ret_gn, rope, ctx_out)
    gx, gc = hgrn2_mixer(px[4:], pc[4:], lb, hg_gn, ctx_out)
    out_x = jnp.concatenate([rx, gx], axis=-1)
    return out_x, (jnp.concatenate([rc, gc], axis=-1) if ctx_out else None)


def setup_inputs(seed: int = 0) -> dict:
    key = jax.random.key(seed)
    ks = iter(jax.random.split(key, 48))

    def nrm(shape, scale):
        return jax.random.normal(next(ks), shape, F32) * scale

    D = D_MODEL
    G, P, K = S5_GROUPS, S5_STATE, S5_GROUP
    return {
        'x': nrm((BATCH, SEQ, D), 1.0),
        'c': nrm((BATCH, D), 1.0),
        'ctx': nrm((BATCH, CTX_LEN, D), 1.0),
        'c_ctx': nrm((D,), 1.0),
        'w_mod': nrm((DEPTH, D, 6 * D), 0.5 * D ** -0.5),
        'b_mod': nrm((DEPTH, 6 * D), 0.01),
        'norm1_g': 1.0 + nrm((DEPTH, D), 0.01),
        'norm2_g': 1.0 + nrm((DEPTH, D), 0.01),
        'ffn_w1': nrm((DEPTH, D, D_FF), D ** -0.5),
        'ffn_w3': nrm((DEPTH, D, D_FF), D ** -0.5),
        'ffn_w2': nrm((DEPTH, D_FF, D), D_FF ** -0.5),
        'w_in_even': nrm((N_EVEN, D, EVEN_IN), D ** -0.5),
        'w_out_even': nrm((N_EVEN, EVEN_MIX, D), EVEN_MIX ** -0.5),
        's5_a_re': -0.5 + nrm((N_EVEN, 2, G, P), 0.01),
        's5_a_im': math.pi * jnp.arange(P, dtype=F32) + nrm((N_EVEN, 2, G, P), 0.01),
        's5_log_dt': jax.random.uniform(next(ks), (N_EVEN, 2, G), F32, math.log(1e-3), math.log(1e-1)),
        's5_b_re': nrm((N_EVEN, 2, G, P, K), (2 * K) ** -0.5),
        's5_b_im': nrm((N_EVEN, 2, G, P, K), (2 * K) ** -0.5),
        's5_c_re': nrm((N_EVEN, 2, G, K, P), (2 * P) ** -0.5),
        's5_c_im': nrm((N_EVEN, 2, G, K, P), (2 * P) ** -0.5),
        's5_d': nrm((N_EVEN, S5_WIDTH), 1.0),
        's5_w_glu': nrm((N_EVEN, S5_WIDTH, S5_WIDTH), S5_WIDTH ** -0.5),
        'mla_q_norm': 1.0 + nrm((N_EVEN, MLA_Q_RANK), 0.01),
        'mla_w_uq': nrm((N_EVEN, MLA_Q_RANK, MLA_HEADS * (MLA_NOPE + MLA_ROPE)), MLA_Q_RANK ** -0.5),
        'mla_kv_norm': 1.0 + nrm((N_EVEN, MLA_KV_RANK), 0.01),
        'mla_w_ukv': nrm((N_EVEN, MLA_KV_RANK, MLA_HEADS * (MLA_NOPE + MLA_V)), MLA_KV_RANK ** -0.5),
        'w_in_odd': nrm((N_ODD, D, ODD_IN), D ** -0.5),
        'w_out_odd': nrm((N_ODD, ODD_MIX, D), ODD_MIX ** -0.5),
        'ret_gn': 1.0 + nrm((N_ODD, RET_VW), 0.01),
        'hg_lb_logits': nrm((N_ODD + 1, HG_QK), 0.1),
        'hg_gn': 1.0 + nrm((N_ODD, HG_VW), 0.01),
        'final_norm': 1.0 + nrm((D,), 0.01),
    }


def reference(x, c, ctx, c_ctx, w_mod, b_mod, norm1_g, norm2_g, ffn_w1, ffn_w3, ffn_w2,
              w_in_even, w_out_even, s5_a_re, s5_a_im, s5_log_dt, s5_b_re, s5_b_im, s5_c_re, s5_c_im,
              s5_d, s5_w_glu, mla_q_norm, mla_w_uq, mla_kv_norm, mla_w_ukv,
              w_in_odd, w_out_odd, ret_gn, hg_lb_logits, hg_gn, final_norm):
    b, l, _ = x.shape
    rows = l // GRID_W
    rope_mla = axial_rope(rows, MLA_ROPE)
    rope_ret = axial_rope(rows, RET_DK)
    hg_lb = jnp.cumsum(jax.nn.softmax(hg_lb_logits.astype(F32), axis=0), axis=0)[:N_ODD]
    for li in range(DEPTH):
        last = li == DEPTH - 1
        mx = (jax.nn.silu(c) @ w_mod[li] + b_mod[li])[:, None, :]
        mc = (jax.nn.silu(c_ctx) @ w_mod[li] + b_mod[li])[None, None, :]
        sx1, ax1, gx1, sx2, ax2, gx2 = jnp.split(mx, 6, axis=-1)
        sc1, ac1, gc1, sc2, ac2, gc2 = jnp.split(mc, 6, axis=-1)
        hx = rms_norm(x, norm1_g[li]) * (1.0 + ax1) + sx1
        hc = rms_norm(ctx, norm1_g[li]) * (1.0 + ac1) + sc1
        j = li // 2
        if li % 2 == 0:
            mix_x, mix_c = even_mixer(hx, hc, w_in_even[j], s5_a_re[j], s5_a_im[j], s5_log_dt[j],
                                      s5_b_re[j], s5_b_im[j], s5_c_re[j], s5_c_im[j], s5_d[j], s5_w_glu[j],
                                      mla_q_norm[j], mla_w_uq[j], mla_kv_norm[j], mla_w_ukv[j],
                                      rope_mla, not last)
            w_out = w_out_even[j]
        else:
            mix_x, mix_c = odd_mixer(hx, hc, w_in_odd[j], ret_gn[j], hg_lb[j], hg_gn[j], rope_ret, not last)
            w_out = w_out_odd[j]
        x = x + gx1 * (mix_x @ w_out)
        x = x + gx2 * swiglu(rms_norm(x, norm2_g[li]) * (1.0 + ax2) + sx2, ffn_w1[li], ffn_w3[li], ffn_w2[li])
        if not last:
            ctx = ctx + gc1 * (mix_c @ w_out)
            ctx = ctx + gc2 * swiglu(rms_norm(ctx, norm2_g[li]) * (1.0 + ac2) + sc2,
                                     ffn_w1[li], ffn_w3[li], ffn_w2[li])
    return rms_norm(x, final_norm)
```

```python
import functools
import math

import jax
import jax.numpy as jnp
import numpy as np
from jax import lax
from jax.experimental import pallas as pl
from jax.experimental.pallas import tpu as pltpu

F32 = jnp.float32
BF16 = jnp.bfloat16

EPS = 1e-6
ROPE_BASE = 10000.0
GRID_W = 64

S5_GROUP = 16
S5_STATE = 64
S5_T = 16

MLA_HEADS = 8
MLA_NOPE = 64
MLA_ROPE = 32
MLA_V = 64
HEAD_PAD = 128

RET_HEADS = 4
HG_HEADS = 4
HEAD_DIM = 128
CHUNK = 128
RET_DECAY_OFFSETS = (0.0, 0.5)
HG_BLOCK = 8

LANES = 128
ROW_TILE_TARGET = 1056
FF_TILE = 256
ATT_TQ = 256
ATT_TK = 256


def _largest_tile(n, target, mult):
    best = None
    for t in range(mult, min(n, target) + 1, mult):
        if n % t == 0:
            best = t
    assert best is not None, (n, target, mult)
    return best


def _sigmoid(v):
    return 1.0 / (1.0 + jnp.exp(-v))


def _dot(a, b):
    return jnp.dot(a, b, preferred_element_type=F32)


def _dot_nt(a, b):
    return lax.dot_general(a, b, (((1,), (1,)), ((), ())), preferred_element_type=F32)


def _dot_tn(a, b):
    return lax.dot_general(a, b, (((0,), (0,)), ((), ())), preferred_element_type=F32)


def _mod_kernel(v_ref, w_ref, b_ref, o_ref):
    v = v_ref[...]
    sv = v * _sigmoid(v)
    o_ref[0] = jnp.dot(sv, w_ref[0], precision=lax.Precision.HIGHEST,
                       preferred_element_type=F32) + b_ref[0]


def _modulation(c, c_ctx, w_mod, b_mod):
    depth, d, n = w_mod.shape
    b = c.shape[0]
    rows = 8 * pl.cdiv(b + 1, 8)
    v = jnp.zeros((rows, d), F32).at[:b].set(c).at[b].set(c_ctx)
    tn = _largest_tile(n, 1536, LANES)
    out = pl.pallas_call(
        _mod_kernel,
        grid=(depth, n // tn),
        in_specs=[pl.BlockSpec((rows, d), lambda l, j: (0, 0)),
                  pl.BlockSpec((1, d, tn), lambda l, j: (l, 0, j)),
                  pl.BlockSpec((1, 1, tn), lambda l, j: (l, 0, j))],
        out_specs=pl.BlockSpec((1, rows, tn), lambda l, j: (l, 0, j)),
        out_shape=jax.ShapeDtypeStruct((depth, rows, n), F32),
    )(v, w_mod, b_mod.reshape(depth, 1, n))
    return out[:, :b + 1].reshape(depth, b + 1, 6, d)


def _modulated_norm(x, g, mx, mc, row0, lc, shift_row, scale_row):
    tm = x.shape[0]
    xn = x * lax.rsqrt(jnp.mean(x * x, axis=-1, keepdims=True) + EPS) * g
    row = row0 + lax.broadcasted_iota(jnp.int32, (tm, 1), 0)
    is_ctx = row < lc
    a = jnp.where(is_ctx, mc[scale_row:scale_row + 1, :], mx[scale_row:scale_row + 1, :])
    s = jnp.where(is_ctx, mc[shift_row:shift_row + 1, :], mx[shift_row:shift_row + 1, :])
    return xn * (1.0 + a) + s


def _in_proj_kernel(x_ref, mx_ref, mc_ref, g_ref, w_ref, cos_ref, sin_ref, o_ref, h_scr,
                    *, lc, tm, n_rope):
    i = pl.program_id(1)
    j = pl.program_id(2)

    @pl.when(j == 0)
    def _():
        h = _modulated_norm(x_ref[0], g_ref[...], mx_ref[0], mc_ref[0], i * tm, lc, 0, 1)
        h_scr[...] = h.astype(BF16)

    y = _dot(h_scr[...], w_ref[...])
    if n_rope == 0:
        o_ref[0] = y
    else:
        @pl.when(j < n_rope)
        def _():
            cos = cos_ref[...]
            sin = sin_ref[...]
            for hh in range(y.shape[1] // HEAD_DIM):
                yh = y[:, hh * HEAD_DIM:(hh + 1) * HEAD_DIM]
                o_ref[0, :, hh * HEAD_DIM:(hh + 1) * HEAD_DIM] = (
                    yh * cos + pltpu.roll(yh, HEAD_DIM // 2, 1) * sin)

        @pl.when(j >= n_rope)
        def _():
            o_ref[0] = y


def _in_proj(xa, mod, g, w, cos_t, sin_t, *, lc, tn, n_rope):
    b, s, d = xa.shape
    n = w.shape[1]
    tm = _largest_tile(s, ROW_TILE_TARGET, 16)
    return pl.pallas_call(
        functools.partial(_in_proj_kernel, lc=lc, tm=tm, n_rope=n_rope),
        grid=(b, s // tm, n // tn),
        in_specs=[pl.BlockSpec((1, tm, d), lambda bb, i, j: (bb, i, 0)),
                  pl.BlockSpec((1, 6, d), lambda bb, i, j: (bb, 0, 0)),
                  pl.BlockSpec((1, 6, d), lambda bb, i, j: (b, 0, 0)),
                  pl.BlockSpec((1, d), lambda bb, i, j: (0, 0)),
                  pl.BlockSpec((d, tn), lambda bb, i, j: (0, j)),
                  pl.BlockSpec((tm, HEAD_DIM), lambda bb, i, j: (i, 0)),
                  pl.BlockSpec((tm, HEAD_DIM), lambda bb, i, j: (i, 0))],
        out_specs=pl.BlockSpec((1, tm, tn), lambda bb, i, j: (bb, i, j)),
        out_shape=jax.ShapeDtypeStruct((b, s, n), F32),
        scratch_shapes=[pltpu.VMEM((tm, d), BF16)],
        compiler_params=pltpu.CompilerParams(
            dimension_semantics=("parallel", "parallel", "arbitrary")),
    )(xa, mod, mod, g.reshape(1, d), w, cos_t, sin_t)


def _out_ffn_kernel(*refs, lc, tm, even, final):
    if even:
        (x_ref, ma_ref, mb_ref, u_ref, mx_ref, mc_ref, g_ref, wo_ref, w1_ref, w3_ref, w2_ref,
         fin_ref, sd_ref, wglu_ref, o_ref, x1_scr, h_scr, acc_scr) = refs
    else:
        (x_ref, ma_ref, mb_ref, mx_ref, mc_ref, g_ref, wo_ref, w1_ref, w3_ref, w2_ref,
         fin_ref, o_ref, x1_scr, h_scr, acc_scr) = refs
    i = pl.program_id(1)
    j = pl.program_id(2)
    half = ma_ref.shape[2]

    def mod_row(r):
        row = i * tm + lax.broadcasted_iota(jnp.int32, (tm, 1), 0)
        return jnp.where(row < lc, mc_ref[0, r:r + 1, :], mx_ref[0, r:r + 1, :])

    @pl.when(j == 0)
    def _():
        if even:
            z = jax.nn.gelu(sd_ref[...] * u_ref[0] + ma_ref[0])
            zb = z.astype(BF16)
            part_a = (z * _sigmoid(_dot(zb, wglu_ref[...]))).astype(BF16)
        else:
            part_a = ma_ref[0].astype(BF16)
        mixed = (_dot(part_a, wo_ref[0:half, :])
                 + _dot(mb_ref[0].astype(BF16), wo_ref[half:2 * half, :]))
        x1 = x_ref[0] + mod_row(2) * mixed
        x1_scr[...] = x1
        h = _modulated_norm(x1, g_ref[...], mx_ref[0], mc_ref[0], i * tm, lc, 3, 4)
        h_scr[...] = h.astype(BF16)
        acc_scr[...] = jnp.zeros_like(acc_scr)

    h = h_scr[...]
    a = _dot(h, w1_ref[...])
    gate = (a * _sigmoid(a)) * _dot(h, w3_ref[...])
    acc_scr[...] += _dot(gate.astype(BF16), w2_ref[...])

    @pl.when(j == pl.num_programs(2) - 1)
    def _():
        x2 = x1_scr[...] + mod_row(5) * acc_scr[...]
        if final:
            x2 = x2 * lax.rsqrt(jnp.mean(x2 * x2, axis=-1, keepdims=True) + EPS) * fin_ref[...]
        o_ref[0] = x2


def _out_ffn(xa, mix_a, mix_b, mod, g, w_out, w1, w3, w2, fin_g, *, lc, final,
             s5_u=None, s5_d=None, w_glu=None):
    b, s, d = xa.shape
    dff = w1.shape[1]
    half = mix_a.shape[2]
    even = s5_u is not None
    tm = _largest_tile(s, ROW_TILE_TARGET, 16)
    tf = _largest_tile(dff, FF_TILE, LANES)
    row_spec = lambda w: pl.BlockSpec((1, tm, w), lambda bb, i, j: (bb, i, 0))
    const2 = lambda shp: pl.BlockSpec(shp, lambda bb, i, j: (0, 0))
    in_specs = [row_spec(d), row_spec(half), row_spec(half)]
    args = [xa, mix_a, mix_b]
    if even:
        in_specs.append(row_spec(half))
        args.append(s5_u)
    in_specs += [pl.BlockSpec((1, 6, d), lambda bb, i, j: (bb, 0, 0)),
                 pl.BlockSpec((1, 6, d), lambda bb, i, j: (b, 0, 0)),
                 const2((1, d)), const2((2 * half, d)),
                 pl.BlockSpec((d, tf), lambda bb, i, j: (0, j)),
                 pl.BlockSpec((d, tf), lambda bb, i, j: (0, j)),
                 pl.BlockSpec((tf, d), lambda bb, i, j: (j, 0)),
                 const2((1, d))]
    args += [mod, mod, g.reshape(1, d), w_out, w1, w3, w2, fin_g.reshape(1, d)]
    if even:
        in_specs += [const2((1, half)), const2((half, half))]
        args += [s5_d.reshape(1, half), w_glu]
    return pl.pallas_call(
        functools.partial(_out_ffn_kernel, lc=lc, tm=tm, even=even, final=final),
        grid=(b, s // tm, dff // tf),
        in_specs=in_specs,
        out_specs=pl.BlockSpec((1, tm, d), lambda bb, i, j: (bb, i, 0)),
        out_shape=jax.ShapeDtypeStruct((b, s, d), F32),
        scratch_shapes=[pltpu.VMEM((tm, d), F32), pltpu.VMEM((tm, d), BF16),
                        pltpu.VMEM((tm, d), F32)],
        compiler_params=pltpu.CompilerParams(
            dimension_semantics=("parallel", "parallel", "arbitrary")),
    )(*args)


def _rope_angles(l, lc, dim):
    rows = l // GRID_W
    r, col = jnp.meshgrid(jnp.arange(rows, dtype=F32), jnp.arange(GRID_W, dtype=F32), indexing='ij')
    quarter = dim // 4
    inv = ROPE_BASE ** (-jnp.arange(quarter, dtype=F32) / quarter)
    ang = jnp.concatenate([r.reshape(-1, 1) * inv, col.reshape(-1, 1) * inv], axis=-1)
    cos = jnp.concatenate([jnp.ones((lc, dim // 2), F32), jnp.cos(ang)], axis=0)
    sin = jnp.concatenate([jnp.zeros((lc, dim // 2), F32), jnp.sin(ang)], axis=0)
    return cos, sin


def _split_pairs_perm(n):
    return np.concatenate([np.arange(0, n, 2), np.arange(1, n, 2)])


def _s5_matrices(a_re, a_im, log_dt, b_re, b_im, c_re, c_im):
    t_len = S5_T
    g, p = a_re.shape[1], a_re.shape[2]
    k = b_re.shape[3]
    tau = jnp.arange(t_len + 1, dtype=F32)
    toep = jnp.zeros((g, t_len, k, t_len, k), F32)
    f_parts, e_parts, a_chunk = [], [], []
    tt = jnp.arange(t_len)
    for r in range(2):
        dt = jnp.exp(log_dt[r])[:, None]
        mag = jnp.exp(a_re[r] * dt)
        ab_re = mag * jnp.cos(a_im[r] * dt)
        ab_im = mag * jnp.sin(a_im[r] * dt)
        nr, ni = ab_re - 1.0, ab_im
        den = a_re[r] * a_re[r] + a_im[r] * a_im[r]
        fr = (nr * a_re[r] + ni * a_im[r]) / den
        fi = (ni * a_re[r] - nr * a_im[r]) / den
        bb_re = fr[..., None] * b_re[r] - fi[..., None] * b_im[r]
        bb_im = fr[..., None] * b_im[r] + fi[..., None] * b_re[r]
        pw_mag = jnp.exp(tau[:, None, None] * (a_re[r] * dt)[None])
        pw_ang = tau[:, None, None] * (a_im[r] * dt)[None]
        pw_re = pw_mag * jnp.cos(pw_ang)
        pw_im = pw_mag * jnp.sin(pw_ang)
        ca_re = c_re[r][None] * pw_re[:, :, None, :] - c_im[r][None] * pw_im[:, :, None, :]
        ca_im = c_re[r][None] * pw_im[:, :, None, :] + c_im[r][None] * pw_re[:, :, None, :]
        m = (jnp.einsum('tgop,gpk->tgok', ca_re, bb_re, precision=lax.Precision.HIGHEST)
             - jnp.einsum('tgop,gpk->tgok', ca_im, bb_im, precision=lax.Precision.HIGHEST))
        lag = (tt[None, :] - tt[:, None]) if r == 0 else (tt[:, None] - tt[None, :])
        valid = lag >= 0
        mg = m[jnp.clip(lag, 0, t_len - 1)]
        mg = jnp.where(valid[:, :, None, None, None], mg, 0.0)
        toep = toep + mg.transpose(2, 0, 4, 1, 3)
        pidx = (t_len - 1 - tt) if r == 0 else tt
        f_re = pw_re[pidx][:, :, :, None] * bb_re[None] - pw_im[pidx][:, :, :, None] * bb_im[None]
        f_im = pw_re[pidx][:, :, :, None] * bb_im[None] + pw_im[pidx][:, :, :, None] * bb_re[None]
        f_parts += [f_re.transpose(1, 0, 3, 2).reshape(g, t_len * k, p),
                    f_im.transpose(1, 0, 3, 2).reshape(g, t_len * k, p)]
        qidx = (tt + 1) if r == 0 else (t_len - tt)
        e_parts += [ca_re[qidx].transpose(1, 3, 0, 2).reshape(g, p, t_len * k),
                    (-ca_im[qidx]).transpose(1, 3, 0, 2).reshape(g, p, t_len * k)]
        a_chunk += [pw_re[t_len].reshape(-1), pw_im[t_len].reshape(-1)]
    toep = toep.reshape(g, t_len * k, t_len * k).astype(BF16)
    pad_lo = lambda m_: jnp.concatenate([m_, jnp.zeros_like(m_)], axis=-1)
    pad_hi = lambda m_: jnp.concatenate([jnp.zeros_like(m_), m_], axis=-1)
    odd = (jnp.arange(g) % 2 == 1)[:, None, None]
    f_pad = jnp.stack([jnp.where(odd, pad_hi(m_), pad_lo(m_)) for m_ in f_parts], axis=1)
    pad_lo_r = lambda m_: jnp.concatenate([m_, jnp.zeros_like(m_)], axis=-2)
    pad_hi_r = lambda m_: jnp.concatenate([jnp.zeros_like(m_), m_], axis=-2)
    e_pad = jnp.stack([jnp.where(odd, pad_hi_r(m_), pad_lo_r(m_)) for m_ in e_parts], axis=1)
    return toep, f_pad.astype(BF16), e_pad.astype(BF16), jnp.stack(a_chunk, axis=0)


def _s5_inject_kernel(u_ref, f_ref, o0, o1, o2, o3):
    u0 = u_ref[0]
    u1 = u_ref[1]
    for a, o in enumerate((o0, o1, o2, o3)):
        o[...] = _dot(u0, f_ref[0, a]) + _dot(u1, f_ref[1, a])


def _s5_scan_kernel(s0, s1, s2, s3, a_ref, h0, h1, h2, h3, *, nb, nc, ncc):
    a = a_ref[...]
    arf, aif, arb, aib = a[0:1], a[1:2], a[2:3], a[3:4]
    ct = s0.shape[1]

    def body(i, carry):
        cb = jnp.where(i < ncc, ncc - 1 - i, nc - 1 - (i - ncc))
        new = []
        for bb in range(nb):
            hr, hi, gr, gi = carry[4 * bb:4 * bb + 4]
            rf = bb * nc + i
            rb = bb * nc + cb
            h0[pl.ds(rf, 1), :] = hr
            h1[pl.ds(rf, 1), :] = hi
            h2[pl.ds(rb, 1), :] = gr
            h3[pl.ds(rb, 1), :] = gi
            new += [arf * hr - aif * hi + s0[pl.ds(rf, 1), :],
                    arf * hi + aif * hr + s1[pl.ds(rf, 1), :],
                    arb * gr - aib * gi + s2[pl.ds(rb, 1), :],
                    arb * gi + aib * gr + s3[pl.ds(rb, 1), :]]
        return tuple(new)

    zero = jnp.zeros((1, ct), F32)
    lax.fori_loop(0, nc, body, (zero,) * (4 * nb))


def _s5_readout_kernel(u_ref, t_ref, h0, h1, h2, h3, e_ref, o_ref):
    y = _dot(u_ref[0], t_ref[0])
    for a, h in enumerate((h0, h1, h2, h3)):
        y = y + _dot(h[...].astype(BF16), e_ref[0, a])
    o_ref[0] = y


def _s5_mixer(proj, mats, *, lc):
    toep, f_pad, e_pad, a_chunk = mats
    b, s, _ = proj.shape
    g = toep.shape[0]
    k = S5_GROUP
    width = g * k
    nc = s // S5_T
    ncc = lc // S5_T
    rows = b * nc
    tk = S5_T * k
    u = proj[:, :, :width].reshape(b, nc, S5_T, g, k).transpose(3, 0, 1, 2, 4)
    u = u.reshape(g, rows, tk).astype(BF16)
    sw = g * S5_STATE
    st_shape = jax.ShapeDtypeStruct((rows, sw), F32)
    inj = pl.pallas_call(
        _s5_inject_kernel,
        grid=(g // 2,),
        in_specs=[pl.BlockSpec((2, rows, tk), lambda j: (j, 0, 0)),
                  pl.BlockSpec((2, 4, tk, LANES), lambda j: (j, 0, 0, 0))],
        out_specs=[pl.BlockSpec((rows, LANES), lambda j: (0, j))] * 4,
        out_shape=[st_shape] * 4,
    )(u, f_pad)
    ct = _largest_tile(sw, 512, LANES)
    states = pl.pallas_call(
        functools.partial(_s5_scan_kernel, nb=b, nc=nc, ncc=ncc),
        grid=(sw // ct,),
        in_specs=[pl.BlockSpec((rows, ct), lambda j: (0, j))] * 4
        + [pl.BlockSpec((4, ct), lambda j: (0, j))],
        out_specs=[pl.BlockSpec((rows, ct), lambda j: (0, j))] * 4,
        out_shape=[st_shape] * 4,
    )(*inj, a_chunk)
    y = pl.pallas_call(
        _s5_readout_kernel,
        grid=(g,),
        in_specs=[pl.BlockSpec((1, rows, tk), lambda j: (j, 0, 0)),
                  pl.BlockSpec((1, tk, tk), lambda j: (j, 0, 0))]
        + [pl.BlockSpec((rows, LANES), lambda j: (0, j // 2))] * 4
        + [pl.BlockSpec((1, 4, LANES, tk), lambda j: (j, 0, 0, 0))],
        out_specs=pl.BlockSpec((1, rows, tk), lambda j: (j, 0, 0)),
        out_shape=jax.ShapeDtypeStruct((g, rows, tk), F32),
    )(u, toep, *states, e_pad)
    y = y.reshape(g, b, nc, S5_T, k).transpose(1, 2, 3, 0, 4)
    return y.reshape(b, s, width)


def _mla_weights(w_uq, w_ukv):
    rq = w_uq.shape[0]
    rkv = w_ukv.shape[0]
    hq = w_uq.reshape(rq, MLA_HEADS, MLA_NOPE + MLA_ROPE)
    nope = hq[:, :, :MLA_NOPE]
    r_even = hq[:, :, MLA_NOPE::2]
    r_odd = hq[:, :, MLA_NOPE + 1::2]
    zq = jnp.zeros((rq, MLA_HEADS, HEAD_PAD - MLA_NOPE - MLA_ROPE), F32)
    wq = jnp.concatenate([nope, r_even, r_odd, zq], axis=-1)
    wq_sw = jnp.concatenate([jnp.zeros_like(nope), r_odd, r_even, zq], axis=-1)
    hkv = w_ukv.reshape(rkv, MLA_HEADS, MLA_NOPE + MLA_V)
    zk = jnp.zeros((rkv, MLA_HEADS, HEAD_PAD - MLA_NOPE), F32)
    wk = jnp.concatenate([hkv[:, :, :MLA_NOPE], zk], axis=-1)
    wv = jnp.concatenate([hkv[:, :, MLA_NOPE:], jnp.zeros((rkv, MLA_HEADS, HEAD_PAD - MLA_V), F32)],
                         axis=-1)
    to_heads = lambda w: w.transpose(1, 0, 2).astype(BF16)
    return to_heads(wq), to_heads(wq_sw), to_heads(wk), to_heads(wv)


def _mla_tables(l, lc):
    cos, sin = _rope_angles(l, lc, MLA_ROPE)
    s = cos.shape[0]
    z_lo = jnp.zeros((s, MLA_NOPE), F32)
    z_hi = jnp.zeros((s, HEAD_PAD - MLA_NOPE - MLA_ROPE), F32)
    ck = jnp.concatenate([z_lo, cos, cos, z_hi], axis=-1)
    sk = jnp.concatenate([z_lo, -sin, sin, z_hi], axis=-1)
    scale = (MLA_NOPE + MLA_ROPE) ** -0.5
    cq = jnp.concatenate([jnp.ones_like(z_lo), cos, cos, z_hi], axis=-1) * scale
    return cq, sk * scale, ck, sk


def _mla_proj_kernel(cq_ref, ckv_ref, kr_ref, krs_ref, qn_ref, kvn_ref, wq_ref, wqs_ref, wk_ref,
                     wv_ref, tcq_ref, tsq_ref, tck_ref, tsk_ref, q_ref, k_ref, v_ref):
    def norm(v, g):
        return (v * lax.rsqrt(jnp.mean(v * v, axis=-1, keepdims=True) + EPS) * g).astype(BF16)

    hq = norm(cq_ref[0], qn_ref[...])
    hkv = norm(ckv_ref[0], kvn_ref[...])
    k_rope = kr_ref[0] * tck_ref[...] + krs_ref[0] * tsk_ref[...]
    lane = lax.broadcasted_iota(jnp.int32, (1, HEAD_PAD), 1)
    ones_col = (lane == MLA_V).astype(F32)
    tcq = tcq_ref[...]
    tsq = tsq_ref[...]
    for hh in range(MLA_HEADS):
        q = _dot(hq, wq_ref[hh]) * tcq + _dot(hq, wqs_ref[hh]) * tsq
        q_ref[0, hh] = q.astype(BF16)
        k_ref[0, hh] = (_dot(hkv, wk_ref[hh]) + k_rope).astype(BF16)
        v_ref[0, hh] = (_dot(hkv, wv_ref[hh]) + ones_col).astype(BF16)


def _attn_kernel(q_ref, k_ref, v_ref, o_ref, *, lc, tq, tk, s_total):
    qi = pl.program_id(2)
    nk = jnp.where(qi * tq < lc, lc // tk, s_total // tk)
    outs = []
    for hh in range(2):
        q = q_ref[0, hh]

        def body(t, carry, hh=hh, q=q):
            m, acc = carry
            start = pl.multiple_of(t * tk, tk)
            kt = k_ref[0, hh, pl.ds(start, tk), :]
            vt = v_ref[0, hh, pl.ds(start, tk), :]
            sc = _dot_nt(q, kt)
            m_new = jnp.maximum(m, jnp.max(sc, axis=-1, keepdims=True))
            p = jnp.exp(sc - m_new)
            acc = acc * jnp.exp(m - m_new) + _dot(p.astype(BF16), vt)
            return m_new, acc

        m0 = jnp.full((tq, 1), -jnp.inf, F32)
        acc0 = jnp.zeros((tq, HEAD_PAD), F32)
        _, acc = lax.fori_loop(0, nk, body, (m0, acc0))
        outs.append(acc[:, :MLA_V] / acc[:, MLA_V:MLA_V + 1])
    o_ref[0] = jnp.concatenate(outs, axis=-1)


def _mla_mixer(proj, q_norm, kv_norm, weights, tables, *, lc):
    b, s, _ = proj.shape
    wq, wqs, wk, wv = weights
    tm = _largest_tile(s, ROW_TILE_TARGET, 16)
    rq = wq.shape[1]
    rkv = wk.shape[1]
    base = 512
    head_shape = jax.ShapeDtypeStruct((b, MLA_HEADS, s, HEAD_PAD), BF16)
    full = lambda shp: pl.BlockSpec(shp, lambda bb, i: tuple(0 for _ in shp))
    tab = pl.BlockSpec((tm, HEAD_PAD), lambda bb, i: (i, 0))
    q, k, v = pl.pallas_call(
        _mla_proj_kernel,
        grid=(b, s // tm),
        in_specs=[pl.BlockSpec((1, tm, rq), lambda bb, i: (bb, i, base // rq)),
                  pl.BlockSpec((1, tm, rkv), lambda bb, i: (bb, i, (base + rq) // rkv)),
                  pl.BlockSpec((1, tm, LANES), lambda bb, i: (bb, i, (base + rq + rkv) // LANES)),
                  pl.BlockSpec((1, tm, LANES), lambda bb, i: (bb, i, (base + rq + rkv) // LANES + 1)),
                  full((1, rq)), full((1, rkv)),
                  full(wq.shape), full(wqs.shape), full(wk.shape), full(wv.shape),
                  tab, tab, tab, tab],
        out_specs=[pl.BlockSpec((1, MLA_HEADS, tm, HEAD_PAD), lambda bb, i: (bb, 0, i, 0))] * 3,
        out_shape=[head_shape] * 3,
    )(proj, proj, proj, proj, q_norm.reshape(1, rq), kv_norm.reshape(1, rkv),
      wq, wqs, wk, wv, *tables)
    tq, tk = ATT_TQ, ATT_TK
    assert lc % tq == 0 and lc % tk == 0 and s % tk == 0
    return pl.pallas_call(
        functools.partial(_attn_kernel, lc=lc, tq=tq, tk=tk, s_total=s),
        grid=(b, MLA_HEADS // 2, s // tq),
        in_specs=[pl.BlockSpec((1, 2, tq, HEAD_PAD), lambda bb, hp, i: (bb, hp, i, 0)),
                  pl.BlockSpec((1, 2, s, HEAD_PAD), lambda bb, hp, i: (bb, hp, 0, 0)),
                  pl.BlockSpec((1, 2, s, HEAD_PAD), lambda bb, hp, i: (bb, hp, 0, 0))],
        out_specs=pl.BlockSpec((1, tq, 2 * MLA_V), lambda bb, hp, i: (bb, i, hp)),
        out_shape=jax.ShapeDtypeStruct((b, s, MLA_HEADS * MLA_V), F32),
        compiler_params=pltpu.CompilerParams(
            dimension_semantics=("parallel", "parallel", "arbitrary")),
    )(q, k, v)


def _state_scan_kernel(kv_ref, dec_ref, st_ref, *, nch, ncc):
    hd = HEAD_DIM

    def body(i, carry):
        sf, sb = carry
        cb = jnp.where(i < ncc, ncc - 1 - i, nch - 1 - (i - ncc))
        st_ref[0, 0, i, :, 0:hd] = sf.astype(BF16)
        st_ref[0, 0, cb, :, hd:2 * hd] = sb.astype(BF16)
        sf = dec_ref[0, 0, i, :, 0:hd] * sf + kv_ref[0, 0, i, :, 0:hd]
        sb = dec_ref[0, 0, cb, :, hd:2 * hd] * sb + kv_ref[0, 0, cb, :, hd:2 * hd]
        return sf, sb

    zero = jnp.zeros((hd, hd), F32)
    lax.fori_loop(0, nch, body, (zero, zero))


def _state_scan(kv, dec, *, ncc):
    b, h, nch = kv.shape[:3]
    hd = HEAD_DIM
    return pl.pallas_call(
        functools.partial(_state_scan_kernel, nch=nch, ncc=ncc),
        grid=(b, h),
        in_specs=[pl.BlockSpec((1, 1, nch, hd, 2 * hd), lambda bb, hh: (bb, hh, 0, 0, 0)),
                  pl.BlockSpec((1, 1, nch, 1, 2 * hd), lambda bb, hh: (bb, hh, 0, 0, 0))],
        out_specs=pl.BlockSpec((1, 1, nch, hd, 2 * hd), lambda bb, hh: (bb, hh, 0, 0, 0)),
        out_shape=jax.ShapeDtypeStruct((b, h, nch, hd, 2 * hd), BF16),
    )(kv, dec)


def _readout_kernel(oi_ref, qd_ref, st_ref, g_ref, gn_ref, o_ref, *, cpt, center):
    def body(c, _):
        r0 = pl.multiple_of(c * CHUNK, CHUNK)
        o = oi_ref[0, 0, pl.ds(r0, CHUNK), :] + _dot_nt(qd_ref[0, 0, pl.ds(r0, CHUNK), :],
                                                        st_ref[0, 0, c])
        if center:
            o = o - jnp.mean(o, axis=-1, keepdims=True)
        o = o * lax.rsqrt(jnp.mean(o * o, axis=-1, keepdims=True) + EPS) * gn_ref[...]
        gate = g_ref[0, pl.ds(r0, CHUNK), :]
        o_ref[0, pl.ds(r0, CHUNK), :] = o * (gate * _sigmoid(gate))
        return 0

    lax.fori_loop(0, cpt, body, 0)


def _readout(o_intra, q_dec, states, proj, gate_col, gn, *, center):
    b, h, s, hd = o_intra.shape
    nch = s // CHUNK
    cpt = _largest_tile(nch, 11, 1)
    tl = cpt * CHUNK
    return pl.pallas_call(
        functools.partial(_readout_kernel, cpt=cpt, center=center),
        grid=(b, h, nch // cpt),
        in_specs=[pl.BlockSpec((1, 1, tl, hd), lambda bb, hh, t: (bb, hh, t, 0)),
                  pl.BlockSpec((1, 1, tl, 2 * hd), lambda bb, hh, t: (bb, hh, t, 0)),
                  pl.BlockSpec((1, 1, cpt, hd, 2 * hd), lambda bb, hh, t: (bb, hh, t, 0, 0)),
                  pl.BlockSpec((1, tl, hd), lambda bb, hh, t: (bb, t, gate_col + hh)),
                  pl.BlockSpec((1, hd), lambda bb, hh, t: (0, hh))],
        out_specs=pl.BlockSpec((1, tl, hd), lambda bb, hh, t: (bb, t, hh)),
        out_shape=jax.ShapeDtypeStruct((b, s, h * hd), F32),
    )(o_intra, q_dec, states, proj, gn.reshape(1, h * hd))


def _local_out_shapes(b, h, s):
    nch = s // CHUNK
    hd = HEAD_DIM
    return [jax.ShapeDtypeStruct((b, h, s, hd), F32),
            jax.ShapeDtypeStruct((b, h, s, 2 * hd), BF16),
            jax.ShapeDtypeStruct((b, h, nch, hd, 2 * hd), F32)]


def _local_out_specs(cpt):
    tl = cpt * CHUNK
    hd = HEAD_DIM
    return [pl.BlockSpec((1, 1, tl, hd), lambda bb, hh, t: (bb, hh, t, 0)),
            pl.BlockSpec((1, 1, tl, 2 * hd), lambda bb, hh, t: (bb, hh, t, 0)),
            pl.BlockSpec((1, 1, cpt, hd, 2 * hd), lambda bb, hh, t: (bb, hh, t, 0, 0))]


def _ret_constants():
    c = CHUNK
    pos = np.arange(c, dtype=np.float64)
    rel = pos[:, None] - pos[None, :]
    scale = HEAD_DIM ** -0.5
    dmask = np.zeros((RET_HEADS, c, c))
    qw = np.zeros((RET_HEADS, c, 2 * HEAD_DIM))
    kw = np.zeros((RET_HEADS, c, 2 * HEAD_DIM))
    dec = np.zeros((RET_HEADS, 1, 2 * HEAD_DIM))
    for r, offset in enumerate(RET_DECAY_OFFSETS):
        lg = np.log1p(-np.exp2(-(5.0 + offset) - np.arange(RET_HEADS, dtype=np.float64)))[:, None, None]
        dist = rel if r == 0 else -rel
        dmask += np.where(dist >= 0, np.exp(lg * np.maximum(dist, 0.0)), 0.0) * scale
        steps_in = (pos + 1) if r == 0 else (c - pos)
        steps_out = (c - 1 - pos) if r == 0 else pos
        sl = slice(r * HEAD_DIM, (r + 1) * HEAD_DIM)
        qw[:, :, sl] = np.exp(lg[:, :, 0] * steps_in[None, :])[:, :, None]
        kw[:, :, sl] = np.exp(lg[:, :, 0] * steps_out[None, :])[:, :, None] * scale
        dec[:, :, sl] = np.exp(lg * c)
    return (jnp.asarray(dmask, F32), jnp.asarray(qw, F32), jnp.asarray(kw, F32), jnp.asarray(dec, F32))


def _ret_local_kernel(q_ref, k_ref, v_ref, dm_ref, qw_ref, kw_ref, oi_ref, qd_ref, kv_ref, *, cpt):
    dm = dm_ref[0]
    qw = qw_ref[0]
    kw = kw_ref[0]

    def body(c, _):
        r0 = pl.multiple_of(c * CHUNK, CHUNK)
        q = q_ref[0, pl.ds(r0, CHUNK), :]
        k = k_ref[0, pl.ds(r0, CHUNK), :]
        v = v_ref[0, pl.ds(r0, CHUNK), :].astype(BF16)
        inner = _dot_nt(q.astype(BF16), k.astype(BF16)) * dm
        oi_ref[0, 0, pl.ds(r0, CHUNK), :] = _dot(inner.astype(BF16), v)
        q2 = jnp.concatenate([q, q], axis=-1) * qw
        qd_ref[0, 0, pl.ds(r0, CHUNK), :] = q2.astype(BF16)
        k2 = (jnp.concatenate([k, k], axis=-1) * kw).astype(BF16)
        kv_ref[0, 0, c] = _dot_tn(v, k2)
        return 0

    lax.fori_loop(0, cpt, body, 0)


def _retention_mixer(proj, gn, consts, *, lc):
    b, s, _ = proj.shape
    h = RET_HEADS
    hd = HEAD_DIM
    dmask, qw, kw, dec = consts
    nch = s // CHUNK
    cpt = _largest_tile(nch, 11, 1)
    tl = cpt * CHUNK
    col = lambda base: pl.BlockSpec((1, tl, hd), lambda bb, hh, t: (bb, t, base + hh))
    per_head = lambda w: pl.BlockSpec((1, CHUNK, w), lambda bb, hh, t: (hh, 0, 0))
    o_intra, q_dec, kv = pl.pallas_call(
        functools.partial(_ret_local_kernel, cpt=cpt),
        grid=(b, h, nch // cpt),
        in_specs=[col(0), col(h), col(2 * h), per_head(CHUNK), per_head(2 * hd), per_head(2 * hd)],
        out_specs=_local_out_specs(cpt),
        out_shape=_local_out_shapes(b, h, s),
    )(proj, proj, proj, dmask, qw, kw)
    dec_all = jnp.broadcast_to(dec[None, :, None], (b, h, nch, 1, 2 * hd))
    states = _state_scan(kv, dec_all, ncc=lc // CHUNK)
    return _readout(o_intra, q_dec, states, proj, 3 * h, gn, center=True)


HG_LEVELS = (8, 16, 32, 64)


def _hg_constants():
    c = CHUNK
    t = np.arange(c)[:, None]
    s = np.arange(c)[None, :]
    sums, masks = [], []
    for blk in HG_LEVELS:
        same = (t // blk) == (s // blk)
        right = (t // blk) % 2 == 1
        sums.append((right & same & (s <= t)) | (~right & same & (s > t)))
        masks.append(((t // (2 * blk)) == (s // (2 * blk))) & right & ((s // blk) % 2 == 0))
    sums.append(np.broadcast_to(s <= t, (c, c)))
    sums.append(np.broadcast_to(s > t, (c, c)))
    sums.append(((t // HG_BLOCK) == (s // HG_BLOCK)) & (s <= t))
    g_f = np.concatenate([m.astype(np.float32) for m in sums], axis=0)
    m_f = np.stack([m.astype(np.float32) for m in masks], axis=0)
    flip = lambda a: a[..., ::-1, ::-1]
    g_b = np.concatenate([flip(m.astype(np.float32)) for m in sums], axis=0)
    m_b = np.stack([flip(m.astype(np.float32)) for m in masks], axis=0)
    return (jnp.asarray(np.stack([g_f, g_b]), BF16), jnp.asarray(np.stack([m_f, m_b]), F32))


def _hg_local_kernel(q_ref, ff_ref, fb_ref, v_ref, lb_ref, gs_ref, mk_ref, oi_ref, qd_ref, kv_ref,
                     dec_ref, *, cpt):
    c_len = CHUNK
    nlev = len(HG_LEVELS)
    lb = lb_ref[...]
    row = lax.broadcasted_iota(jnp.int32, (c_len, c_len), 0)
    col = lax.broadcasted_iota(jnp.int32, (c_len, c_len), 1)
    row_in_blk = row % HG_BLOCK

    def body(c, _):
        r0 = pl.multiple_of(c * c_len, c_len)
        q = q_ref[0, pl.ds(r0, c_len), :]
        v = v_ref[0, pl.ds(r0, c_len), :].astype(BF16)
        att = jnp.zeros((c_len, c_len), F32)
        q_parts, kv_parts, dec_parts = [], [], []
        for d, f_ref in enumerate((ff_ref, fb_ref)):
            f = lb + (1.0 - lb) * _sigmoid(f_ref[0, pl.ds(r0, c_len), :])
            kk = 1.0 - f
            z = _dot(gs_ref[d], jnp.log(f).astype(BF16))
            for lv in range(nlev):
                e = jnp.exp(z[lv * c_len:(lv + 1) * c_len])
                att = att + mk_ref[d, lv] * _dot_nt((q * e).astype(BF16), (kk * e).astype(BF16))
            zq = z[nlev * c_len:(nlev + 1) * c_len]
            zk = z[(nlev + 1) * c_len:(nlev + 2) * c_len]
            cl = z[(nlev + 2) * c_len:(nlev + 3) * c_len]
            for delta in range(HG_BLOCK):
                if delta == 0:
                    prod = q * kk
                else:
                    sh = delta if d == 0 else c_len - delta
                    prod = (q * pltpu.roll(kk, sh, 0)
                            * jnp.exp(jnp.minimum(cl - pltpu.roll(cl, sh, 0), 0.0)))
                a = jnp.sum(prod, axis=-1, keepdims=True)
                if d == 0:
                    hit = (col == row - delta) & (row_in_blk >= delta)
                else:
                    hit = (col == row + delta) & (row_in_blk + delta < HG_BLOCK)
                att = att + jnp.where(hit, a, 0.0)
            q_parts.append(q * jnp.exp(zq))
            kv_parts.append(kk * jnp.exp(zk))
            last = c_len - 1 if d == 0 else 0
            dec_parts.append(jnp.exp(zq[last:last + 1]))
        oi_ref[0, 0, pl.ds(r0, c_len), :] = _dot(att.astype(BF16), v)
        qd_ref[0, 0, pl.ds(r0, c_len), :] = jnp.concatenate(q_parts, axis=-1).astype(BF16)
        kv_ref[0, 0, c] = _dot_tn(v, jnp.concatenate(kv_parts, axis=-1).astype(BF16))
        dec_ref[0, 0, c] = jnp.concatenate(dec_parts, axis=-1)
        return 0

    lax.fori_loop(0, cpt, body, 0)


def _hgrn2_mixer(proj, lb, gn, consts, *, lc, col0):
    b, s, _ = proj.shape
    h = HG_HEADS
    hd = HEAD_DIM
    gsum, masks = consts
    nch = s // CHUNK
    cpt = _largest_tile(nch, 11, 1)
    tl = cpt * CHUNK
    col = lambda base: pl.BlockSpec((1, tl, hd), lambda bb, hh, t: (bb, t, col0 + base + hh))
    full = lambda a: pl.BlockSpec(a.shape, lambda bb, hh, t: tuple(0 for _ in a.shape))
    o_intra, q_dec, kv, dec = pl.pallas_call(
        functools.partial(_hg_local_kernel, cpt=cpt),
        grid=(b, h, nch // cpt),
        in_specs=[col(0), col(h), col(2 * h), col(3 * h),
                  pl.BlockSpec((1, hd), lambda bb, hh, t: (0, hh)), full(gsum), full(masks)],
        out_specs=_local_out_specs(cpt)
        + [pl.BlockSpec((1, 1, cpt, 1, 2 * hd), lambda bb, hh, t: (bb, hh, t, 0, 0))],
        out_shape=_local_out_shapes(b, h, s)
        + [jax.ShapeDtypeStruct((b, h, nch, 1, 2 * hd), F32)],
    )(proj, proj, proj, proj, lb.reshape(1, h * hd), gsum, masks)
    states = _state_scan(kv, dec, ncc=lc // CHUNK)
    return _readout(o_intra, q_dec, states, proj, col0 + 4 * h, gn, center=False)


def kernel(x, c, ctx, c_ctx, w_mod, b_mod, norm1_g, norm2_g, ffn_w1, ffn_w3, ffn_w2, w_in_even, w_out_even, s5_a_re, s5_a_im, s5_log_dt, s5_b_re, s5_b_im, s5_c_re, s5_c_im, s5_d, s5_w_glu, mla_q_norm, mla_w_uq, mla_kv_norm, mla_w_ukv, w_in_odd, w_out_odd, ret_gn, hg_lb_logits, hg_gn, final_norm):
    b, l, d = x.shape
    lc = ctx.shape[1]
    depth = w_mod.shape[0]
    n_odd = w_in_odd.shape[0]
    assert lc % CHUNK == 0 and l % CHUNK == 0 and l % GRID_W == 0

    xa = jnp.concatenate([ctx, x], axis=1)
    mods = _modulation(c, c_ctx, w_mod, b_mod)

    cos_r, sin_r = _rope_angles(l, lc, HEAD_DIM)
    ret_cos = jnp.concatenate([cos_r, cos_r], axis=-1)
    ret_sin = jnp.concatenate([-sin_r, sin_r], axis=-1)
    mla_tabs = _mla_tables(l, lc)
    ret_consts = _ret_constants()
    hg_consts = _hg_constants()
    hg_lb = jnp.cumsum(jax.nn.softmax(hg_lb_logits.astype(F32), axis=0), axis=0)[:n_odd]

    s5_w = s5_d.shape[1]
    kr0 = w_in_even.shape[2] - MLA_ROPE
    kr_e = w_in_even[:, :, kr0::2]
    kr_o = w_in_even[:, :, kr0 + 1::2]
    z_lo = jnp.zeros(w_in_even.shape[:2] + (MLA_NOPE,), F32)
    z_hi = jnp.zeros(w_in_even.shape[:2] + (HEAD_PAD - MLA_NOPE - MLA_ROPE,), F32)
    w_in_e = jnp.concatenate([w_in_even[:, :, :kr0], z_lo, kr_e, kr_o, z_hi, z_lo, kr_o, kr_e, z_hi],
                             axis=-1).astype(BF16)
    qk_w = 2 * RET_HEADS * HEAD_DIM
    perm = (np.arange(qk_w) // HEAD_DIM) * HEAD_DIM + np.tile(_split_pairs_perm(HEAD_DIM),
                                                               qk_w // HEAD_DIM)
    w_in_o = jnp.concatenate([w_in_odd[:, :, perm], w_in_odd[:, :, qk_w:]], axis=-1).astype(BF16)

    w1 = ffn_w1.astype(BF16)
    w3 = ffn_w3.astype(BF16)
    w2 = ffn_w2.astype(BF16)
    w_out_e = w_out_even.astype(BF16)
    w_out_o = w_out_odd.astype(BF16)
    w_glu = s5_w_glu.astype(BF16)

    for li in range(depth):
        j = li // 2
        final = li == depth - 1
        if li % 2 == 0:
            proj = _in_proj(xa, mods[li], norm1_g[li], w_in_e[j], ret_cos, ret_sin,
                            lc=lc, tn=w_in_e.shape[2], n_rope=0)
            mats = _s5_matrices(s5_a_re[j], s5_a_im[j], s5_log_dt[j], s5_b_re[j], s5_b_im[j],
                                s5_c_re[j], s5_c_im[j])
            y_s5 = _s5_mixer(proj, mats, lc=lc)
            attn = _mla_mixer(proj, mla_q_norm[j], mla_kv_norm[j],
                              _mla_weights(mla_w_uq[j], mla_w_ukv[j]), mla_tabs, lc=lc)
            xa = _out_ffn(xa, y_s5, attn, mods[li], norm2_g[li], w_out_e[j], w1[li], w3[li], w2[li],
                          final_norm, lc=lc, final=final, s5_u=proj, s5_d=s5_d[j], w_glu=w_glu[j])
        else:
            proj = _in_proj(xa, mods[li], norm1_g[li], w_in_o[j], ret_cos, ret_sin,
                            lc=lc, tn=RET_HEADS * HEAD_DIM, n_rope=2)
            ret = _retention_mixer(proj, ret_gn[j], ret_consts, lc=lc)
            hg = _hgrn2_mixer(proj, hg_lb[j], hg_gn[j], hg_consts, lc=lc, col0=4 * RET_HEADS)
            xa = _out_ffn(xa, ret, hg, mods[li], norm2_g[li], w_out_o[j], w1[li], w3[li], w2[li],
                          final_norm, lc=lc, final=final)
    return xa[:, lc:, :]
```

```python
import functools
import math

import jax
import jax.numpy as jnp
import numpy as np
from jax import lax
from jax.experimental import pallas as pl
from jax.experimental.pallas import tpu as pltpu

F32 = jnp.float32
BF16 = jnp.bfloat16

EPS = 1e-6
ROPE_BASE = 10000.0
GRID_W = 64

S5_GROUP = 16
S5_STATE = 64
S5_T = 16

MLA_HEADS = 8
MLA_NOPE = 64
MLA_ROPE = 32
MLA_V = 64
HEAD_PAD = 128

RET_HEADS = 4
HG_HEADS = 4
HEAD_DIM = 128
CHUNK = 128
RET_DECAY_OFFSETS = (0.0, 0.5)
HG_BLOCK = 8

LANES = 128
ROW_TILE_TARGET = 1056
FF_TILE = 256
ATT_TQ = 256
ATT_TK = 768
ATT_SCORE_CHUNK = 128
ATT_PROB_CHUNK = 256


def _largest_tile(n, target, mult):
    best = None
    for t in range(mult, min(n, target) + 1, mult):
        if n % t == 0:
            best = t
    assert best is not None, (n, target, mult)
    return best


def _sigmoid(v):
    return 1.0 / (1.0 + jnp.exp(-v))


def _dot(a, b):
    return jnp.dot(a, b, preferred_element_type=F32)


def _dot_nt(a, b):
    return lax.dot_general(a, b, (((1,), (1,)), ((), ())), preferred_element_type=F32)


def _dot_tn(a, b):
    return lax.dot_general(a, b, (((0,), (0,)), ((), ())), preferred_element_type=F32)


def _mod_kernel(v_ref, w_ref, b_ref, o_ref):
    v = v_ref[...]
    sv = v * _sigmoid(v)
    o_ref[0] = jnp.dot(sv, w_ref[0], precision=lax.Precision.HIGHEST,
                       preferred_element_type=F32) + b_ref[0]


def _modulation(c, c_ctx, w_mod, b_mod):
    depth, d, n = w_mod.shape
    b = c.shape[0]
    rows = 8 * pl.cdiv(b + 1, 8)
    v = jnp.zeros((rows, d), F32).at[:b].set(c).at[b].set(c_ctx)
    tn = _largest_tile(n, 1536, LANES)
    out = pl.pallas_call(
        _mod_kernel,
        grid=(depth, n // tn),
        in_specs=[pl.BlockSpec((rows, d), lambda l, j: (0, 0)),
                  pl.BlockSpec((1, d, tn), lambda l, j: (l, 0, j)),
                  pl.BlockSpec((1, 1, tn), lambda l, j: (l, 0, j))],
        out_specs=pl.BlockSpec((1, rows, tn), lambda l, j: (l, 0, j)),
        out_shape=jax.ShapeDtypeStruct((depth, rows, n), F32),
    )(v, w_mod, b_mod.reshape(depth, 1, n))
    return out[:, :b + 1].reshape(depth, b + 1, 6, d)


def _modulated_norm(x, g, mx, mc, row0, lc, shift_row, scale_row):
    tm = x.shape[0]
    xn = x * lax.rsqrt(jnp.mean(x * x, axis=-1, keepdims=True) + EPS) * g
    row = row0 + lax.broadcasted_iota(jnp.int32, (tm, 1), 0)
    is_ctx = row < lc
    a = jnp.where(is_ctx, mc[scale_row:scale_row + 1, :], mx[scale_row:scale_row + 1, :])
    s = jnp.where(is_ctx, mc[shift_row:shift_row + 1, :], mx[shift_row:shift_row + 1, :])
    return xn * (1.0 + a) + s


def _in_proj_kernel(x_ref, mx_ref, mc_ref, g_ref, w_ref, cos_ref, sin_ref, o_ref, h_scr,
                    *, lc, tm, n_rope):
    i = pl.program_id(1)
    j = pl.program_id(2)

    @pl.when(j == 0)
    def _():
        h = _modulated_norm(x_ref[0], g_ref[...], mx_ref[0], mc_ref[0], i * tm, lc, 0, 1)
        h_scr[...] = h.astype(BF16)

    y = _dot(h_scr[...], w_ref[...])
    if n_rope == 0:
        o_ref[0] = y
    else:
        @pl.when(j < n_rope)
        def _():
            cos = cos_ref[...]
            sin = sin_ref[...]
            for hh in range(y.shape[1] // HEAD_DIM):
                yh = y[:, hh * HEAD_DIM:(hh + 1) * HEAD_DIM]
                o_ref[0, :, hh * HEAD_DIM:(hh + 1) * HEAD_DIM] = (
                    yh * cos + pltpu.roll(yh, HEAD_DIM // 2, 1) * sin)

        @pl.when(j >= n_rope)
        def _():
            o_ref[0] = y


def _in_proj(xa, mod, g, w, cos_t, sin_t, *, lc, tn, n_rope):
    b, s, d = xa.shape
    n = w.shape[1]
    tm = _largest_tile(s, ROW_TILE_TARGET, 16)
    return pl.pallas_call(
        functools.partial(_in_proj_kernel, lc=lc, tm=tm, n_rope=n_rope),
        grid=(b, s // tm, n // tn),
        in_specs=[pl.BlockSpec((1, tm, d), lambda bb, i, j: (bb, i, 0)),
                  pl.BlockSpec((1, 6, d), lambda bb, i, j: (bb, 0, 0)),
                  pl.BlockSpec((1, 6, d), lambda bb, i, j: (b, 0, 0)),
                  pl.BlockSpec((1, d), lambda bb, i, j: (0, 0)),
                  pl.BlockSpec((d, tn), lambda bb, i, j: (0, j)),
                  pl.BlockSpec((tm, HEAD_DIM), lambda bb, i, j: (i, 0)),
                  pl.BlockSpec((tm, HEAD_DIM), lambda bb, i, j: (i, 0))],
        out_specs=pl.BlockSpec((1, tm, tn), lambda bb, i, j: (bb, i, j)),
        out_shape=jax.ShapeDtypeStruct((b, s, n), F32),
        scratch_shapes=[pltpu.VMEM((tm, d), BF16)],
        compiler_params=pltpu.CompilerParams(
            dimension_semantics=("parallel", "parallel", "arbitrary")),
    )(xa, mod, mod, g.reshape(1, d), w, cos_t, sin_t)


def _out_ffn_kernel(*refs, lc, tm, even, final):
    if even:
        (x_ref, ma_ref, mb_ref, u_ref, mx_ref, mc_ref, g_ref, wo_ref, w1_ref, w3_ref, w2_ref,
         fin_ref, sd_ref, wglu_ref, o_ref, x1_scr, h_scr, acc_scr) = refs
    else:
        (x_ref, ma_ref, mb_ref, mx_ref, mc_ref, g_ref, wo_ref, w1_ref, w3_ref, w2_ref,
         fin_ref, o_ref, x1_scr, h_scr, acc_scr) = refs
    i = pl.program_id(1)
    j = pl.program_id(2)
    half = ma_ref.shape[2]

    def mod_row(r):
        row = i * tm + lax.broadcasted_iota(jnp.int32, (tm, 1), 0)
        return jnp.where(row < lc, mc_ref[0, r:r + 1, :], mx_ref[0, r:r + 1, :])

    @pl.when(j == 0)
    def _():
        if even:
            z = jax.nn.gelu(sd_ref[...] * u_ref[0] + ma_ref[0])
            zb = z.astype(BF16)
            part_a = (z * _sigmoid(_dot(zb, wglu_ref[...]))).astype(BF16)
        else:
            part_a = ma_ref[0].astype(BF16)
        mixed = (_dot(part_a, wo_ref[0:half, :])
                 + _dot(mb_ref[0].astype(BF16), wo_ref[half:2 * half, :]))
        x1 = x_ref[0] + mod_row(2) * mixed
        x1_scr[...] = x1
        h = _modulated_norm(x1, g_ref[...], mx_ref[0], mc_ref[0], i * tm, lc, 3, 4)
        h_scr[...] = h.astype(BF16)
        acc_scr[...] = jnp.zeros_like(acc_scr)

    h = h_scr[...]
    a = _dot(h, w1_ref[...])
    gate = (a * _sigmoid(a)) * _dot(h, w3_ref[...])
    acc_scr[...] += _dot(gate.astype(BF16), w2_ref[...])

    @pl.when(j == pl.num_programs(2) - 1)
    def _():
        x2 = x1_scr[...] + mod_row(5) * acc_scr[...]
        if final:
            x2 = x2 * lax.rsqrt(jnp.mean(x2 * x2, axis=-1, keepdims=True) + EPS) * fin_ref[...]
        o_ref[0] = x2


def _out_ffn(xa, mix_a, mix_b, mod, g, w_out, w1, w3, w2, fin_g, *, lc, final,
             s5_u=None, s5_d=None, w_glu=None):
    b, s, d = xa.shape
    dff = w1.shape[1]
    half = mix_a.shape[2]
    even = s5_u is not None
    tm = _largest_tile(s, ROW_TILE_TARGET, 16)
    tf = _largest_tile(dff, FF_TILE, LANES)
    row_spec = lambda w: pl.BlockSpec((1, tm, w), lambda bb, i, j: (bb, i, 0))
    const2 = lambda shp: pl.BlockSpec(shp, lambda bb, i, j: (0, 0))
    in_specs = [row_spec(d), row_spec(half), row_spec(half)]
    args = [xa, mix_a, mix_b]
    if even:
        in_specs.append(row_spec(half))
        args.append(s5_u)
    in_specs += [pl.BlockSpec((1, 6, d), lambda bb, i, j: (bb, 0, 0)),
                 pl.BlockSpec((1, 6, d), lambda bb, i, j: (b, 0, 0)),
                 const2((1, d)), const2((2 * half, d)),
                 pl.BlockSpec((d, tf), lambda bb, i, j: (0, j)),
                 pl.BlockSpec((d, tf), lambda bb, i, j: (0, j)),
                 pl.BlockSpec((tf, d), lambda bb, i, j: (j, 0)),
                 const2((1, d))]
    args += [mod, mod, g.reshape(1, d), w_out, w1, w3, w2, fin_g.reshape(1, d)]
    if even:
        in_specs += [const2((1, half)), const2((half, half))]
        args += [s5_d.reshape(1, half), w_glu]
    return pl.pallas_call(
        functools.partial(_out_ffn_kernel, lc=lc, tm=tm, even=even, final=final),
        grid=(b, s // tm, dff // tf),
        in_specs=in_specs,
        out_specs=pl.BlockSpec((1, tm, d), lambda bb, i, j: (bb, i, 0)),
        out_shape=jax.ShapeDtypeStruct((b, s, d), F32),
        scratch_shapes=[pltpu.VMEM((tm, d), F32), pltpu.VMEM((tm, d), BF16),
                        pltpu.VMEM((tm, d), F32)],
        compiler_params=pltpu.CompilerParams(
            dimension_semantics=("parallel", "parallel", "arbitrary")),
    )(*args)


def _rope_angles(l, lc, dim):
    rows = l // GRID_W
    r, col = jnp.meshgrid(jnp.arange(rows, dtype=F32), jnp.arange(GRID_W, dtype=F32), indexing='ij')
    quarter = dim // 4
    inv = ROPE_BASE ** (-jnp.arange(quarter, dtype=F32) / quarter)
    ang = jnp.concatenate([r.reshape(-1, 1) * inv, col.reshape(-1, 1) * inv], axis=-1)
    cos = jnp.concatenate([jnp.ones((lc, dim // 2), F32), jnp.cos(ang)], axis=0)
    sin = jnp.concatenate([jnp.zeros((lc, dim // 2), F32), jnp.sin(ang)], axis=0)
    return cos, sin


def _split_pairs_perm(n):
    return np.concatenate([np.arange(0, n, 2), np.arange(1, n, 2)])


def _s5_matrices(a_re, a_im, log_dt, b_re, b_im, c_re, c_im):
    t_len = S5_T
    g, p = a_re.shape[1], a_re.shape[2]
    k = b_re.shape[3]
    tau = jnp.arange(t_len + 1, dtype=F32)
    toep = jnp.zeros((g, t_len, k, t_len, k), F32)
    f_parts, e_parts, a_chunk = [], [], []
    tt = jnp.arange(t_len)
    for r in range(2):
        dt = jnp.exp(log_dt[r])[:, None]
        mag = jnp.exp(a_re[r] * dt)
        ab_re = mag * jnp.cos(a_im[r] * dt)
        ab_im = mag * jnp.sin(a_im[r] * dt)
        nr, ni = ab_re - 1.0, ab_im
        den = a_re[r] * a_re[r] + a_im[r] * a_im[r]
        fr = (nr * a_re[r] + ni * a_im[r]) / den
        fi = (ni * a_re[r] - nr * a_im[r]) / den
        bb_re = fr[..., None] * b_re[r] - fi[..., None] * b_im[r]
        bb_im = fr[..., None] * b_im[r] + fi[..., None] * b_re[r]
        pw_mag = jnp.exp(tau[:, None, None] * (a_re[r] * dt)[None])
        pw_ang = tau[:, None, None] * (a_im[r] * dt)[None]
        pw_re = pw_mag * jnp.cos(pw_ang)
        pw_im = pw_mag * jnp.sin(pw_ang)
        ca_re = c_re[r][None] * pw_re[:, :, None, :] - c_im[r][None] * pw_im[:, :, None, :]
        ca_im = c_re[r][None] * pw_im[:, :, None, :] + c_im[r][None] * pw_re[:, :, None, :]
        m = (jnp.einsum('tgop,gpk->tgok', ca_re, bb_re, precision=lax.Precision.HIGHEST)
             - jnp.einsum('tgop,gpk->tgok', ca_im, bb_im, precision=lax.Precision.HIGHEST))
        lag = (tt[None, :] - tt[:, None]) if r == 0 else (tt[:, None] - tt[None, :])
        valid = lag >= 0
        mg = m[jnp.clip(lag, 0, t_len - 1)]
        mg = jnp.where(valid[:, :, None, None, None], mg, 0.0)
        toep = toep + mg.transpose(2, 0, 4, 1, 3)
        pidx = (t_len - 1 - tt) if r == 0 else tt
        f_re = pw_re[pidx][:, :, :, None] * bb_re[None] - pw_im[pidx][:, :, :, None] * bb_im[None]
        f_im = pw_re[pidx][:, :, :, None] * bb_im[None] + pw_im[pidx][:, :, :, None] * bb_re[None]
        f_parts += [f_re.transpose(1, 0, 3, 2).reshape(g, t_len * k, p),
                    f_im.transpose(1, 0, 3, 2).reshape(g, t_len * k, p)]
        qidx = (tt + 1) if r == 0 else (t_len - tt)
        e_parts += [ca_re[qidx].transpose(1, 3, 0, 2).reshape(g, p, t_len * k),
                    (-ca_im[qidx]).transpose(1, 3, 0, 2).reshape(g, p, t_len * k)]
        a_chunk += [pw_re[t_len].reshape(-1), pw_im[t_len].reshape(-1)]
    toep = toep.reshape(g, t_len * k, t_len * k).astype(BF16)
    pad_lo = lambda m_: jnp.concatenate([m_, jnp.zeros_like(m_)], axis=-1)
    pad_hi = lambda m_: jnp.concatenate([jnp.zeros_like(m_), m_], axis=-1)
    odd = (jnp.arange(g) % 2 == 1)[:, None, None]
    f_pad = jnp.stack([jnp.where(odd, pad_hi(m_), pad_lo(m_)) for m_ in f_parts], axis=1)
    pad_lo_r = lambda m_: jnp.concatenate([m_, jnp.zeros_like(m_)], axis=-2)
    pad_hi_r = lambda m_: jnp.concatenate([jnp.zeros_like(m_), m_], axis=-2)
    e_pad = jnp.stack([jnp.where(odd, pad_hi_r(m_), pad_lo_r(m_)) for m_ in e_parts], axis=1)
    return toep, f_pad.astype(BF16), e_pad.astype(BF16), jnp.stack(a_chunk, axis=0)


def _s5_inject_kernel(u_ref, f_ref, o0, o1, o2, o3):
    u0 = u_ref[0]
    u1 = u_ref[1]
    for a, o in enumerate((o0, o1, o2, o3)):
        o[...] = _dot(u0, f_ref[0, a]) + _dot(u1, f_ref[1, a])


def _s5_scan_kernel(s0, s1, s2, s3, a_ref, h0, h1, h2, h3, *, nb, nc, ncc):
    a = a_ref[...]
    arf, aif, arb, aib = a[0:1], a[1:2], a[2:3], a[3:4]
    ct = s0.shape[1]

    def body(i, carry):
        cb = jnp.where(i < ncc, ncc - 1 - i, nc - 1 - (i - ncc))
        new = []
        for bb in range(nb):
            hr, hi, gr, gi = carry[4 * bb:4 * bb + 4]
            rf = bb * nc + i
            rb = bb * nc + cb
            h0[pl.ds(rf, 1), :] = hr
            h1[pl.ds(rf, 1), :] = hi
            h2[pl.ds(rb, 1), :] = gr
            h3[pl.ds(rb, 1), :] = gi
            new += [arf * hr - aif * hi + s0[pl.ds(rf, 1), :],
                    arf * hi + aif * hr + s1[pl.ds(rf, 1), :],
                    arb * gr - aib * gi + s2[pl.ds(rb, 1), :],
                    arb * gi + aib * gr + s3[pl.ds(rb, 1), :]]
        return tuple(new)

    zero = jnp.zeros((1, ct), F32)
    lax.fori_loop(0, nc, body, (zero,) * (4 * nb))


def _s5_readout_kernel(u_ref, t_ref, h0, h1, h2, h3, e_ref, o_ref):
    y = _dot(u_ref[0], t_ref[0])
    for a, h in enumerate((h0, h1, h2, h3)):
        y = y + _dot(h[...].astype(BF16), e_ref[0, a])
    o_ref[0] = y


def _s5_mixer(proj, mats, *, lc):
    toep, f_pad, e_pad, a_chunk = mats
    b, s, _ = proj.shape
    g = toep.shape[0]
    k = S5_GROUP
    width = g * k
    nc = s // S5_T
    ncc = lc // S5_T
    rows = b * nc
    tk = S5_T * k
    u = proj[:, :, :width].reshape(b, nc, S5_T, g, k).transpose(3, 0, 1, 2, 4)
    u = u.reshape(g, rows, tk).astype(BF16)
    sw = g * S5_STATE
    st_shape = jax.ShapeDtypeStruct((rows, sw), F32)
    inj = pl.pallas_call(
        _s5_inject_kernel,
        grid=(g // 2,),
        in_specs=[pl.BlockSpec((2, rows, tk), lambda j: (j, 0, 0)),
                  pl.BlockSpec((2, 4, tk, LANES), lambda j: (j, 0, 0, 0))],
        out_specs=[pl.BlockSpec((rows, LANES), lambda j: (0, j))] * 4,
        out_shape=[st_shape] * 4,
    )(u, f_pad)
    ct = _largest_tile(sw, 512, LANES)
    states = pl.pallas_call(
        functools.partial(_s5_scan_kernel, nb=b, nc=nc, ncc=ncc),
        grid=(sw // ct,),
        in_specs=[pl.BlockSpec((rows, ct), lambda j: (0, j))] * 4
        + [pl.BlockSpec((4, ct), lambda j: (0, j))],
        out_specs=[pl.BlockSpec((rows, ct), lambda j: (0, j))] * 4,
        out_shape=[st_shape] * 4,
    )(*inj, a_chunk)
    y = pl.pallas_call(
        _s5_readout_kernel,
        grid=(g,),
        in_specs=[pl.BlockSpec((1, rows, tk), lambda j: (j, 0, 0)),
                  pl.BlockSpec((1, tk, tk), lambda j: (j, 0, 0))]
        + [pl.BlockSpec((rows, LANES), lambda j: (0, j // 2))] * 4
        + [pl.BlockSpec((1, 4, LANES, tk), lambda j: (j, 0, 0, 0))],
        out_specs=pl.BlockSpec((1, rows, tk), lambda j: (j, 0, 0)),
        out_shape=jax.ShapeDtypeStruct((g, rows, tk), F32),
    )(u, toep, *states, e_pad)
    y = y.reshape(g, b, nc, S5_T, k).transpose(1, 2, 3, 0, 4)
    return y.reshape(b, s, width)


def _mla_weights(w_uq, w_ukv):
    rq = w_uq.shape[0]
    rkv = w_ukv.shape[0]
    hq = w_uq.reshape(rq, MLA_HEADS, MLA_NOPE + MLA_ROPE)
    nope = hq[:, :, :MLA_NOPE]
    r_even = hq[:, :, MLA_NOPE::2]
    r_odd = hq[:, :, MLA_NOPE + 1::2]
    zq = jnp.zeros((rq, MLA_HEADS, HEAD_PAD - MLA_NOPE - MLA_ROPE), F32)
    wq = jnp.concatenate([nope, r_even, r_odd, zq], axis=-1)
    wq_sw = jnp.concatenate([jnp.zeros_like(nope), r_odd, r_even, zq], axis=-1)
    hkv = w_ukv.reshape(rkv, MLA_HEADS, MLA_NOPE + MLA_V)
    zk = jnp.zeros((rkv, MLA_HEADS, HEAD_PAD - MLA_NOPE), F32)
    wk = jnp.concatenate([hkv[:, :, :MLA_NOPE], zk], axis=-1)
    wv = jnp.concatenate([hkv[:, :, MLA_NOPE:], jnp.zeros((rkv, MLA_HEADS, HEAD_PAD - MLA_V), F32)],
                         axis=-1)
    out_in = lambda w: w.transpose(1, 2, 0).astype(BF16)
    return out_in(wq), out_in(wq_sw), wk.transpose(1, 0, 2).astype(BF16), out_in(wv)


def _mla_tables(l, lc):
    cos, sin = _rope_angles(l, lc, MLA_ROPE)
    s = cos.shape[0]
    z_lo = jnp.zeros((s, MLA_NOPE), F32)
    z_hi = jnp.zeros((s, HEAD_PAD - MLA_NOPE - MLA_ROPE), F32)
    ck = jnp.concatenate([z_lo, cos, cos, z_hi], axis=-1)
    sk = jnp.concatenate([z_lo, -sin, sin, z_hi], axis=-1)
    scale = (MLA_NOPE + MLA_ROPE) ** -0.5 * math.log2(math.e)
    cq = jnp.concatenate([jnp.ones_like(z_lo), cos, cos, z_hi], axis=-1) * scale
    return cq.T, sk.T * scale, ck, sk


def _mla_proj_kernel(cq_ref, ckv_ref, kr_ref, krs_ref, qn_ref, kvn_ref, wqt_ref, wqst_ref, wk_ref,
                     wvt_ref, tcq_ref, tsq_ref, tck_ref, tsk_ref, qt_ref, k_ref, vt_ref):
    def norm(v, g):
        return (v * lax.rsqrt(jnp.mean(v * v, axis=-1, keepdims=True) + EPS) * g).astype(BF16)

    hq = norm(cq_ref[0], qn_ref[...])
    hkv = norm(ckv_ref[0], kvn_ref[...])
    k_rope = kr_ref[0] * tck_ref[...] + krs_ref[0] * tsk_ref[...]
    row = lax.broadcasted_iota(jnp.int32, (HEAD_PAD, 1), 0)
    ones_row = (row == MLA_V).astype(F32)
    tcq = tcq_ref[...]
    tsq = tsq_ref[...]
    for hh in range(MLA_HEADS):
        qt = _dot_nt(wqt_ref[hh], hq) * tcq + _dot_nt(wqst_ref[hh], hq) * tsq
        qt_ref[0, hh] = qt.astype(BF16)
        k_ref[0, hh] = (_dot(hkv, wk_ref[hh]) + k_rope).astype(BF16)
        vt_ref[0, hh, 0] = (_dot_nt(wvt_ref[hh], hkv) + ones_row).astype(BF16)


def _attn_kernel(qt_ref, k_ref, vt_ref, o_ref, st_scr, acc_scr, *, lc, tq, tk, s_total):
    heads = qt_ref.shape[1]
    kc = ATT_SCORE_CHUNK
    pc = ATT_PROB_CHUNK
    sub = 8

    def score_chunk(slot, hh, first, c, part):
        row0 = first + c * kc
        if not isinstance(row0, int):
            row0 = pl.multiple_of(row0, kc)
        st = _dot(k_ref[0, hh, pl.ds(row0, kc), :], qt_ref[0, hh])
        st_scr[slot, hh, c * kc:(c + 1) * kc, :] = st
        cm = jnp.max(st.reshape(kc // sub, sub, tq), axis=0)
        return cm if part is None else jnp.maximum(part, cm)

    def prob_chunk(slot, hh, tile, j, m, alpha):
        p = jnp.exp2(st_scr[slot, hh, j * pc:(j + 1) * pc, :] - m).astype(BF16)
        pv = _dot(vt_ref[0, hh, tile, :, j * pc:(j + 1) * pc], p)
        if j == 0:
            acc_scr[hh] = acc_scr[hh] * alpha + pv
        else:
            acc_scr[hh] += pv

    def fold_max(m_old, part):
        m_new = jnp.maximum(m_old, jnp.max(part, axis=0, keepdims=True))
        return m_new, jnp.exp2(m_old - m_new)

    def phase(nkeys, a_args, bc_args):
        parts = [None] * heads
        assert pc // kc == heads and nkeys % pc == 0
        for j in range(nkeys // pc):
            for r in range(heads):
                if a_args is not None:
                    for hh in range(heads):
                        parts[hh] = score_chunk(a_args[0], hh, a_args[1], j * heads + r, parts[hh])
                if bc_args is not None:
                    slot, tile, ms, alphas = bc_args
                    prob_chunk(slot, r, tile, j, ms[r], alphas[r])
        return parts

    def finish():
        out_t = jnp.concatenate(
            [acc_scr[hh, 0:MLA_V, :] / acc_scr[hh, MLA_V:MLA_V + 1, :] for hh in range(heads)],
            axis=0)
        o_ref[0] = out_t.T

    def attend(nkeys, n_tiles):
        acc_scr[...] = jnp.zeros(acc_scr.shape, F32)
        neg = jnp.full((1, tq), -jnp.inf, F32)
        parts = phase(nkeys, (0, 0), None)
        state = []
        for hh in range(heads):
            state += list(fold_max(neg, parts[hh]))

        def step(t, slot, carry):
            ms, alphas = carry[0::2], carry[1::2]
            nxt = phase(nkeys, (1 - slot, (t + 1) * nkeys), (slot, t, ms, alphas))
            new = []
            for hh in range(heads):
                new += list(fold_max(ms[hh], nxt[hh]))
            return tuple(new)

        pairs = (n_tiles - 1) // 2
        state = tuple(state)
        if pairs > 0:
            state = lax.fori_loop(
                0, pairs, lambda i, carry: step(2 * i + 1, 1, step(2 * i, 0, carry)), state)
        for t in range(2 * pairs, n_tiles - 1):
            state = step(t, t % 2, state)
        phase(nkeys, None, ((n_tiles - 1) % 2, n_tiles - 1, state[0::2], state[1::2]))
        finish()

    is_ctx = pl.program_id(2) < lc // tq

    @pl.when(is_ctx)
    def _():
        attend(lc, 1)

    @pl.when(jnp.logical_not(is_ctx))
    def _():
        attend(tk, s_total // tk)


def _mla_mixer(proj, q_norm, kv_norm, weights, tables, *, lc):
    b, s, _ = proj.shape
    wqt, wqst, wk, wvt = weights
    tq = ATT_TQ
    tk = _largest_tile(s, ATT_TK, ATT_PROB_CHUNK)
    assert lc % tq == 0 and s % tq == 0 and lc <= tk and lc % ATT_PROB_CHUNK == 0
    rq = wqt.shape[2]
    rkv = wk.shape[1]
    base = 512
    head_shape = jax.ShapeDtypeStruct((b, MLA_HEADS, s, HEAD_PAD), BF16)
    full = lambda shp: pl.BlockSpec(shp, lambda bb, i: tuple(0 for _ in shp))
    tab = pl.BlockSpec((tk, HEAD_PAD), lambda bb, i: (i, 0))
    tab_t = pl.BlockSpec((HEAD_PAD, tk), lambda bb, i: (0, i))
    head_rows = pl.BlockSpec((1, MLA_HEADS, tk, HEAD_PAD), lambda bb, i: (bb, 0, i, 0))
    qt, k, vt = pl.pallas_call(
        _mla_proj_kernel,
        grid=(b, s // tk),
        in_specs=[pl.BlockSpec((1, tk, rq), lambda bb, i: (bb, i, base // rq)),
                  pl.BlockSpec((1, tk, rkv), lambda bb, i: (bb, i, (base + rq) // rkv)),
                  pl.BlockSpec((1, tk, LANES), lambda bb, i: (bb, i, (base + rq + rkv) // LANES)),
                  pl.BlockSpec((1, tk, LANES), lambda bb, i: (bb, i, (base + rq + rkv) // LANES + 1)),
                  full((1, rq)), full((1, rkv)),
                  full(wqt.shape), full(wqst.shape), full(wk.shape), full(wvt.shape),
                  tab_t, tab_t, tab, tab],
        out_specs=[pl.BlockSpec((1, MLA_HEADS, HEAD_PAD, tk), lambda bb, i: (bb, 0, 0, i)), head_rows,
                   pl.BlockSpec((1, MLA_HEADS, 1, HEAD_PAD, tk), lambda bb, i: (bb, 0, i, 0, 0))],
        out_shape=[jax.ShapeDtypeStruct((b, MLA_HEADS, HEAD_PAD, s), BF16), head_shape,
                   jax.ShapeDtypeStruct((b, MLA_HEADS, s // tk, HEAD_PAD, tk), BF16)],
    )(proj, proj, proj, proj, q_norm.reshape(1, rq), kv_norm.reshape(1, rkv),
      wqt, wqst, wk, wvt, *tables)
    return pl.pallas_call(
        functools.partial(_attn_kernel, lc=lc, tq=tq, tk=tk, s_total=s),
        grid=(b, MLA_HEADS // 2, s // tq),
        in_specs=[pl.BlockSpec((1, 2, HEAD_PAD, tq), lambda bb, hp, i: (bb, hp, 0, i)),
                  pl.BlockSpec((1, 2, s, HEAD_PAD), lambda bb, hp, i: (bb, hp, 0, 0)),
                  pl.BlockSpec((1, 2, s // tk, HEAD_PAD, tk), lambda bb, hp, i: (bb, hp, 0, 0, 0))],
        out_specs=pl.BlockSpec((1, tq, 2 * MLA_V), lambda bb, hp, i: (bb, i, hp)),
        out_shape=jax.ShapeDtypeStruct((b, s, MLA_HEADS * MLA_V), F32),
        scratch_shapes=[pltpu.VMEM((2, 2, tk, tq), F32), pltpu.VMEM((2, HEAD_PAD, tq), F32)],
        compiler_params=pltpu.CompilerParams(
            dimension_semantics=("parallel", "parallel", "arbitrary")),
    )(qt, k, vt)


def _state_scan_kernel(kv_ref, dec_ref, st_ref, *, nch, ncc):
    hd = HEAD_DIM

    def body(i, carry):
        sf, sb = carry
        cb = jnp.where(i < ncc, ncc - 1 - i, nch - 1 - (i - ncc))
        st_ref[0, 0, i, :, 0:hd] = sf.astype(BF16)
        st_ref[0, 0, cb, :, hd:2 * hd] = sb.astype(BF16)
        sf = dec_ref[0, 0, i, :, 0:hd] * sf + kv_ref[0, 0, i, :, 0:hd]
        sb = dec_ref[0, 0, cb, :, hd:2 * hd] * sb + kv_ref[0, 0, cb, :, hd:2 * hd]
        return sf, sb

    zero = jnp.zeros((hd, hd), F32)
    lax.fori_loop(0, nch, body, (zero, zero))


def _state_scan(kv, dec, *, ncc):
    b, h, nch = kv.shape[:3]
    hd = HEAD_DIM
    return pl.pallas_call(
        functools.partial(_state_scan_kernel, nch=nch, ncc=ncc),
        grid=(b, h),
        in_specs=[pl.BlockSpec((1, 1, nch, hd, 2 * hd), lambda bb, hh: (bb, hh, 0, 0, 0)),
                  pl.BlockSpec((1, 1, nch, 1, 2 * hd), lambda bb, hh: (bb, hh, 0, 0, 0))],
        out_specs=pl.BlockSpec((1, 1, nch, hd, 2 * hd), lambda bb, hh: (bb, hh, 0, 0, 0)),
        out_shape=jax.ShapeDtypeStruct((b, h, nch, hd, 2 * hd), BF16),
    )(kv, dec)


def _readout_kernel(oi_ref, qd_ref, st_ref, g_ref, gn_ref, o_ref, *, cpt, center):
    def body(c, _):
        r0 = pl.multiple_of(c * CHUNK, CHUNK)
        o = oi_ref[0, 0, pl.ds(r0, CHUNK), :] + _dot_nt(qd_ref[0, 0, pl.ds(r0, CHUNK), :],
                                                        st_ref[0, 0, c])
        if center:
            o = o - jnp.mean(o, axis=-1, keepdims=True)
        o = o * lax.rsqrt(jnp.mean(o * o, axis=-1, keepdims=True) + EPS) * gn_ref[...]
        gate = g_ref[0, pl.ds(r0, CHUNK), :]
        o_ref[0, pl.ds(r0, CHUNK), :] = o * (gate * _sigmoid(gate))
        return 0

    lax.fori_loop(0, cpt, body, 0)


def _readout(o_intra, q_dec, states, proj, gate_col, gn, *, center):
    b, h, s, hd = o_intra.shape
    nch = s // CHUNK
    cpt = _largest_tile(nch, 11, 1)
    tl = cpt * CHUNK
    return pl.pallas_call(
        functools.partial(_readout_kernel, cpt=cpt, center=center),
        grid=(b, h, nch // cpt),
        in_specs=[pl.BlockSpec((1, 1, tl, hd), lambda bb, hh, t: (bb, hh, t, 0)),
                  pl.BlockSpec((1, 1, tl, 2 * hd), lambda bb, hh, t: (bb, hh, t, 0)),
                  pl.BlockSpec((1, 1, cpt, hd, 2 * hd), lambda bb, hh, t: (bb, hh, t, 0, 0)),
                  pl.BlockSpec((1, tl, hd), lambda bb, hh, t: (bb, t, gate_col + hh)),
                  pl.BlockSpec((1, hd), lambda bb, hh, t: (0, hh))],
        out_specs=pl.BlockSpec((1, tl, hd), lambda bb, hh, t: (bb, t, hh)),
        out_shape=jax.ShapeDtypeStruct((b, s, h * hd), F32),
    )(o_intra, q_dec, states, proj, gn.reshape(1, h * hd))


def _local_out_shapes(b, h, s):
    nch = s // CHUNK
    hd = HEAD_DIM
    return [jax.ShapeDtypeStruct((b, h, s, hd), F32),
            jax.ShapeDtypeStruct((b, h, s, 2 * hd), BF16),
            jax.ShapeDtypeStruct((b, h, nch, hd, 2 * hd), F32)]


def _local_out_specs(cpt):
    tl = cpt * CHUNK
    hd = HEAD_DIM
    return [pl.BlockSpec((1, 1, tl, hd), lambda bb, hh, t: (bb, hh, t, 0)),
            pl.BlockSpec((1, 1, tl, 2 * hd), lambda bb, hh, t: (bb, hh, t, 0)),
            pl.BlockSpec((1, 1, cpt, hd, 2 * hd), lambda bb, hh, t: (bb, hh, t, 0, 0))]


def _ret_constants():
    c = CHUNK
    pos = np.arange(c, dtype=np.float64)
    rel = pos[:, None] - pos[None, :]
    scale = HEAD_DIM ** -0.5
    dmask = np.zeros((RET_HEADS, c, c))
    qw = np.zeros((RET_HEADS, c, 2 * HEAD_DIM))
    kw = np.zeros((RET_HEADS, c, 2 * HEAD_DIM))
    dec = np.zeros((RET_HEADS, 1, 2 * HEAD_DIM))
    for r, offset in enumerate(RET_DECAY_OFFSETS):
        lg = np.log1p(-np.exp2(-(5.0 + offset) - np.arange(RET_HEADS, dtype=np.float64)))[:, None, None]
        dist = rel if r == 0 else -rel
        dmask += np.where(dist >= 0, np.exp(lg * np.maximum(dist, 0.0)), 0.0) * scale
        steps_in = (pos + 1) if r == 0 else (c - pos)
        steps_out = (c - 1 - pos) if r == 0 else pos
        sl = slice(r * HEAD_DIM, (r + 1) * HEAD_DIM)
        qw[:, :, sl] = np.exp(lg[:, :, 0] * steps_in[None, :])[:, :, None]
        kw[:, :, sl] = np.exp(lg[:, :, 0] * steps_out[None, :])[:, :, None] * scale
        dec[:, :, sl] = np.exp(lg * c)
    return (jnp.asarray(dmask, F32), jnp.asarray(qw, F32), jnp.asarray(kw, F32), jnp.asarray(dec, F32))


def _ret_local_kernel(q_ref, k_ref, v_ref, dm_ref, qw_ref, kw_ref, oi_ref, qd_ref, kv_ref, *, cpt):
    dm = dm_ref[0]
    qw = qw_ref[0]
    kw = kw_ref[0]

    def body(c, _):
        r0 = pl.multiple_of(c * CHUNK, CHUNK)
        q = q_ref[0, pl.ds(r0, CHUNK), :]
        k = k_ref[0, pl.ds(r0, CHUNK), :]
        v = v_ref[0, pl.ds(r0, CHUNK), :].astype(BF16)
        inner = _dot_nt(q.astype(BF16), k.astype(BF16)) * dm
        oi_ref[0, 0, pl.ds(r0, CHUNK), :] = _dot(inner.astype(BF16), v)
        q2 = jnp.concatenate([q, q], axis=-1) * qw
        qd_ref[0, 0, pl.ds(r0, CHUNK), :] = q2.astype(BF16)
        k2 = (jnp.concatenate([k, k], axis=-1) * kw).astype(BF16)
        kv_ref[0, 0, c] = _dot_tn(v, k2)
        return 0

    lax.fori_loop(0, cpt, body, 0)


def _retention_mixer(proj, gn, consts, *, lc):
    b, s, _ = proj.shape
    h = RET_HEADS
    hd = HEAD_DIM
    dmask, qw, kw, dec = consts
    nch = s // CHUNK
    cpt = _largest_tile(nch, 11, 1)
    tl = cpt * CHUNK
    col = lambda base: pl.BlockSpec((1, tl, hd), lambda bb, hh, t: (bb, t, base + hh))
    per_head = lambda w: pl.BlockSpec((1, CHUNK, w), lambda bb, hh, t: (hh, 0, 0))
    o_intra, q_dec, kv = pl.pallas_call(
        functools.partial(_ret_local_kernel, cpt=cpt),
        grid=(b, h, nch // cpt),
        in_specs=[col(0), col(h), col(2 * h), per_head(CHUNK), per_head(2 * hd), per_head(2 * hd)],
        out_specs=_local_out_specs(cpt),
        out_shape=_local_out_shapes(b, h, s),
    )(proj, proj, proj, dmask, qw, kw)
    dec_all = jnp.broadcast_to(dec[None, :, None], (b, h, nch, 1, 2 * hd))
    states = _state_scan(kv, dec_all, ncc=lc // CHUNK)
    return _readout(o_intra, q_dec, states, proj, 3 * h, gn, center=True)


HG_LEVELS = (8, 16, 32, 64)


def _hg_constants():
    c = CHUNK
    t = np.arange(c)[:, None]
    s = np.arange(c)[None, :]
    sums, masks = [], []
    for blk in HG_LEVELS:
        same = (t // blk) == (s // blk)
        right = (t // blk) % 2 == 1
        sums.append((right & same & (s <= t)) | (~right & same & (s > t)))
        masks.append(((t // (2 * blk)) == (s // (2 * blk))) & right & ((s // blk) % 2 == 0))
    sums.append(np.broadcast_to(s <= t, (c, c)))
    sums.append(np.broadcast_to(s > t, (c, c)))
    sums.append(((t // HG_BLOCK) == (s // HG_BLOCK)) & (s <= t))
    g_f = np.concatenate([m.astype(np.float32) for m in sums], axis=0)
    m_f = np.stack([m.astype(np.float32) for m in masks], axis=0)
    flip = lambda a: a[..., ::-1, ::-1]
    g_b = np.concatenate([flip(m.astype(np.float32)) for m in sums], axis=0)
    m_b = np.stack([flip(m.astype(np.float32)) for m in masks], axis=0)
    return (jnp.asarray(np.stack([g_f, g_b]), BF16), jnp.asarray(np.stack([m_f, m_b]), F32))


def _hg_local_kernel(q_ref, ff_ref, fb_ref, v_ref, lb_ref, gs_ref, mk_ref, oi_ref, qd_ref, kv_ref,
                     dec_ref, *, cpt):
    c_len = CHUNK
    nlev = len(HG_LEVELS)
    lb = lb_ref[...]
    row = lax.broadcasted_iota(jnp.int32, (c_len, c_len), 0)
    col = lax.broadcasted_iota(jnp.int32, (c_len, c_len), 1)
    row_in_blk = row % HG_BLOCK

    def body(c, _):
        r0 = pl.multiple_of(c * c_len, c_len)
        q = q_ref[0, pl.ds(r0, c_len), :]
        v = v_ref[0, pl.ds(r0, c_len), :].astype(BF16)
        att = jnp.zeros((c_len, c_len), F32)
        q_parts, kv_parts, dec_parts = [], [], []
        for d, f_ref in enumerate((ff_ref, fb_ref)):
            f = lb + (1.0 - lb) * _sigmoid(f_ref[0, pl.ds(r0, c_len), :])
            kk = 1.0 - f
            z = _dot(gs_ref[d], jnp.log(f).astype(BF16))
            for lv in range(nlev):
                e = jnp.exp(z[lv * c_len:(lv + 1) * c_len])
                att = att + mk_ref[d, lv] * _dot_nt((q * e).astype(BF16), (kk * e).astype(BF16))
            zq = z[nlev * c_len:(nlev + 1) * c_len]
            zk = z[(nlev + 1) * c_len:(nlev + 2) * c_len]
            cl = z[(nlev + 2) * c_len:(nlev + 3) * c_len]
            for delta in range(HG_BLOCK):
                if delta == 0:
                    prod = q * kk
                else:
                    sh = delta if d == 0 else c_len - delta
                    prod = (q * pltpu.roll(kk, sh, 0)
                            * jnp.exp(jnp.minimum(cl - pltpu.roll(cl, sh, 0), 0.0)))
                a = jnp.sum(prod, axis=-1, keepdims=True)
                if d == 0:
                    hit = (col == row - delta) & (row_in_blk >= delta)
                else:
                    hit = (col == row + delta) & (row_in_blk + delta < HG_BLOCK)
                att = att + jnp.where(hit, a, 0.0)
            q_parts.append(q * jnp.exp(zq))
            kv_parts.append(kk * jnp.exp(zk))
            last = c_len - 1 if d == 0 else 0
            dec_parts.append(jnp.exp(zq[last:last + 1]))
        oi_ref[0, 0, pl.ds(r0, c_len), :] = _dot(att.astype(BF16), v)
        qd_ref[0, 0, pl.ds(r0, c_len), :] = jnp.concatenate(q_parts, axis=-1).astype(BF16)
        kv_ref[0, 0, c] = _dot_tn(v, jnp.concatenate(kv_parts, axis=-1).astype(BF16))
        dec_ref[0, 0, c] = jnp.concatenate(dec_parts, axis=-1)
        return 0

    lax.fori_loop(0, cpt, body, 0)


def _hgrn2_mixer(proj, lb, gn, consts, *, lc, col0):
    b, s, _ = proj.shape
    h = HG_HEADS
    hd = HEAD_DIM
    gsum, masks = consts
    nch = s // CHUNK
    cpt = _largest_tile(nch, 11, 1)
    tl = cpt * CHUNK
    col = lambda base: pl.BlockSpec((1, tl, hd), lambda bb, hh, t: (bb, t, col0 + base + hh))
    full = lambda a: pl.BlockSpec(a.shape, lambda bb, hh, t: tuple(0 for _ in a.shape))
    o_intra, q_dec, kv, dec = pl.pallas_call(
        functools.partial(_hg_local_kernel, cpt=cpt),
        grid=(b, h, nch // cpt),
        in_specs=[col(0), col(h), col(2 * h), col(3 * h),
                  pl.BlockSpec((1, hd), lambda bb, hh, t: (0, hh)), full(gsum), full(masks)],
        out_specs=_local_out_specs(cpt)
        + [pl.BlockSpec((1, 1, cpt, 1, 2 * hd), lambda bb, hh, t: (bb, hh, t, 0, 0))],
        out_shape=_local_out_shapes(b, h, s)
        + [jax.ShapeDtypeStruct((b, h, nch, 1, 2 * hd), F32)],
    )(proj, proj, proj, proj, lb.reshape(1, h * hd), gsum, masks)
    states = _state_scan(kv, dec, ncc=lc // CHUNK)
    return _readout(o_intra, q_dec, states, proj, col0 + 4 * h, gn, center=False)


def kernel(x, c, ctx, c_ctx, w_mod, b_mod, norm1_g, norm2_g, ffn_w1, ffn_w3, ffn_w2, w_in_even, w_out_even, s5_a_re, s5_a_im, s5_log_dt, s5_b_re, s5_b_im, s5_c_re, s5_c_im, s5_d, s5_w_glu, mla_q_norm, mla_w_uq, mla_kv_norm, mla_w_ukv, w_in_odd, w_out_odd, ret_gn, hg_lb_logits, hg_gn, final_norm):
    b, l, d = x.shape
    lc = ctx.shape[1]
    depth = w_mod.shape[0]
    n_odd = w_in_odd.shape[0]
    assert lc % CHUNK == 0 and l % CHUNK == 0 and l % GRID_W == 0

    xa = jnp.concatenate([ctx, x], axis=1)
    mods = _modulation(c, c_ctx, w_mod, b_mod)

    cos_r, sin_r = _rope_angles(l, lc, HEAD_DIM)
    ret_cos = jnp.concatenate([cos_r, cos_r], axis=-1)
    ret_sin = jnp.concatenate([-sin_r, sin_r], axis=-1)
    mla_tabs = _mla_tables(l, lc)
    ret_consts = _ret_constants()
    hg_consts = _hg_constants()
    hg_lb = jnp.cumsum(jax.nn.softmax(hg_lb_logits.astype(F32), axis=0), axis=0)[:n_odd]

    s5_w = s5_d.shape[1]
    kr0 = w_in_even.shape[2] - MLA_ROPE
    kr_e = w_in_even[:, :, kr0::2]
    kr_o = w_in_even[:, :, kr0 + 1::2]
    z_lo = jnp.zeros(w_in_even.shape[:2] + (MLA_NOPE,), F32)
    z_hi = jnp.zeros(w_in_even.shape[:2] + (HEAD_PAD - MLA_NOPE - MLA_ROPE,), F32)
    w_in_e = jnp.concatenate([w_in_even[:, :, :kr0], z_lo, kr_e, kr_o, z_hi, z_lo, kr_o, kr_e, z_hi],
                             axis=-1).astype(BF16)
    qk_w = 2 * RET_HEADS * HEAD_DIM
    perm = (np.arange(qk_w) // HEAD_DIM) * HEAD_DIM + np.tile(_split_pairs_perm(HEAD_DIM),
                                                               qk_w // HEAD_DIM)
    w_in_o = jnp.concatenate([w_in_odd[:, :, perm], w_in_odd[:, :, qk_w:]], axis=-1).astype(BF16)

    w1 = ffn_w1.astype(BF16)
    w3 = ffn_w3.astype(BF16)
    w2 = ffn_w2.astype(BF16)
    w_out_e = w_out_even.astype(BF16)
    w_out_o = w_out_odd.astype(BF16)
    w_glu = s5_w_glu.astype(BF16)

    for li in range(depth):
        j = li // 2
        final = li == depth - 1
        if li % 2 == 0:
            proj = _in_proj(xa, mods[li], norm1_g[li], w_in_e[j], ret_cos, ret_sin,
                            lc=lc, tn=w_in_e.shape[2], n_rope=0)
            mats = _s5_matrices(s5_a_re[j], s5_a_im[j], s5_log_dt[j], s5_b_re[j], s5_b_im[j],
                                s5_c_re[j], s5_c_im[j])
            y_s5 = _s5_mixer(proj, mats, lc=lc)
            attn = _mla_mixer(proj, mla_q_norm[j], mla_kv_norm[j],
                              _mla_weights(mla_w_uq[j], mla_w_ukv[j]), mla_tabs, lc=lc)
            xa = _out_ffn(xa, y_s5, attn, mods[li], norm2_g[li], w_out_e[j], w1[li], w3[li], w2[li],
                          final_norm, lc=lc, final=final, s5_u=proj, s5_d=s5_d[j], w_glu=w_glu[j])
        else:
            proj = _in_proj(xa, mods[li], norm1_g[li], w_in_o[j], ret_cos, ret_sin,
                            lc=lc, tn=RET_HEADS * HEAD_DIM, n_rope=2)
            ret = _retention_mixer(proj, ret_gn[j], ret_consts, lc=lc)
            hg = _hgrn2_mixer(proj, hg_lb[j], hg_gn[j], hg_consts, lc=lc, col0=4 * RET_HEADS)
            xa = _out_ffn(xa, ret, hg, mods[li], norm2_g[li], w_out_o[j], w1[li], w3[li], w2[li],
                          final_norm, lc=lc, final=final)
    return xa[:, lc:, :]
```

```python
import functools
import math

import jax
import jax.numpy as jnp
import numpy as np
from jax import lax
from jax.experimental import pallas as pl
from jax.experimental.pallas import tpu as pltpu

F32 = jnp.float32
BF16 = jnp.bfloat16

EPS = 1e-6
ROPE_BASE = 10000.0
GRID_W = 64

S5_GROUP = 16
S5_STATE = 64
S5_T = 16

MLA_HEADS = 8
MLA_NOPE = 64
MLA_ROPE = 32
MLA_V = 64
HEAD_PAD = 128

RET_HEADS = 4
HG_HEADS = 4
HEAD_DIM = 128
CHUNK = 128
RET_DECAY_OFFSETS = (0.0, 0.5)
HG_BLOCK = 1

LANES = 128
ROW_TILE_TARGET = 1056
FF_TILE = 256
ATT_TQ = 256
ATT_TK = 768
ATT_SCORE_CHUNK = 128
ATT_PROB_CHUNK = 256


def _largest_tile(n, target, mult):
    best = None
    for t in range(mult, min(n, target) + 1, mult):
        if n % t == 0:
            best = t
    assert best is not None, (n, target, mult)
    return best


def _sigmoid(v):
    return 1.0 / (1.0 + jnp.exp(-v))


def _dot(a, b):
    return jnp.dot(a, b, preferred_element_type=F32)


def _dot_nt(a, b):
    return lax.dot_general(a, b, (((1,), (1,)), ((), ())), preferred_element_type=F32)


def _dot_tn(a, b):
    return lax.dot_general(a, b, (((0,), (0,)), ((), ())), preferred_element_type=F32)


def _mod_kernel(v_ref, w_ref, b_ref, o_ref):
    v = v_ref[...]
    sv = v * _sigmoid(v)
    o_ref[0] = jnp.dot(sv, w_ref[0], precision=lax.Precision.HIGHEST,
                       preferred_element_type=F32) + b_ref[0]


def _modulation(c, c_ctx, w_mod, b_mod):
    depth, d, n = w_mod.shape
    b = c.shape[0]
    rows = 8 * pl.cdiv(b + 1, 8)
    v = jnp.zeros((rows, d), F32).at[:b].set(c).at[b].set(c_ctx)
    tn = _largest_tile(n, 1536, LANES)
    out = pl.pallas_call(
        _mod_kernel,
        grid=(depth, n // tn),
        in_specs=[pl.BlockSpec((rows, d), lambda l, j: (0, 0)),
                  pl.BlockSpec((1, d, tn), lambda l, j: (l, 0, j)),
                  pl.BlockSpec((1, 1, tn), lambda l, j: (l, 0, j))],
        out_specs=pl.BlockSpec((1, rows, tn), lambda l, j: (l, 0, j)),
        out_shape=jax.ShapeDtypeStruct((depth, rows, n), F32),
    )(v, w_mod, b_mod.reshape(depth, 1, n))
    return out[:, :b + 1].reshape(depth, b + 1, 6, d)


def _modulated_norm(x, g, mx, mc, row0, lc, shift_row, scale_row):
    tm = x.shape[0]
    xn = x * lax.rsqrt(jnp.mean(x * x, axis=-1, keepdims=True) + EPS) * g
    row = row0 + lax.broadcasted_iota(jnp.int32, (tm, 1), 0)
    is_ctx = row < lc
    a = jnp.where(is_ctx, mc[scale_row:scale_row + 1, :], mx[scale_row:scale_row + 1, :])
    s = jnp.where(is_ctx, mc[shift_row:shift_row + 1, :], mx[shift_row:shift_row + 1, :])
    return xn * (1.0 + a) + s


def _in_proj_kernel(x_ref, mx_ref, mc_ref, g_ref, w_ref, cos_ref, sin_ref, o_ref, h_scr,
                    *, lc, tm, n_rope):
    i = pl.program_id(1)
    j = pl.program_id(2)

    @pl.when(j == 0)
    def _():
        h = _modulated_norm(x_ref[0], g_ref[...], mx_ref[0], mc_ref[0], i * tm, lc, 0, 1)
        h_scr[...] = h.astype(BF16)

    y = _dot(h_scr[...], w_ref[...])
    if n_rope == 0:
        o_ref[0] = y
    else:
        @pl.when(j < n_rope)
        def _():
            cos = cos_ref[...]
            sin = sin_ref[...]
            for hh in range(y.shape[1] // HEAD_DIM):
                yh = y[:, hh * HEAD_DIM:(hh + 1) * HEAD_DIM]
                o_ref[0, :, hh * HEAD_DIM:(hh + 1) * HEAD_DIM] = (
                    yh * cos + pltpu.roll(yh, HEAD_DIM // 2, 1) * sin)

        @pl.when(j >= n_rope)
        def _():
            o_ref[0] = y


def _in_proj(xa, mod, g, w, cos_t, sin_t, *, lc, tn, n_rope):
    b, s, d = xa.shape
    n = w.shape[1]
    tm = _largest_tile(s, ROW_TILE_TARGET, 16)
    return pl.pallas_call(
        functools.partial(_in_proj_kernel, lc=lc, tm=tm, n_rope=n_rope),
        grid=(b, s // tm, n // tn),
        in_specs=[pl.BlockSpec((1, tm, d), lambda bb, i, j: (bb, i, 0)),
                  pl.BlockSpec((1, 6, d), lambda bb, i, j: (bb, 0, 0)),
                  pl.BlockSpec((1, 6, d), lambda bb, i, j: (b, 0, 0)),
                  pl.BlockSpec((1, d), lambda bb, i, j: (0, 0)),
                  pl.BlockSpec((d, tn), lambda bb, i, j: (0, j)),
                  pl.BlockSpec((tm, HEAD_DIM), lambda bb, i, j: (i, 0)),
                  pl.BlockSpec((tm, HEAD_DIM), lambda bb, i, j: (i, 0))],
        out_specs=pl.BlockSpec((1, tm, tn), lambda bb, i, j: (bb, i, j)),
        out_shape=jax.ShapeDtypeStruct((b, s, n), F32),
        scratch_shapes=[pltpu.VMEM((tm, d), BF16)],
        compiler_params=pltpu.CompilerParams(
            dimension_semantics=("parallel", "parallel", "arbitrary")),
    )(xa, mod, mod, g.reshape(1, d), w, cos_t, sin_t)


def _out_ffn_kernel(*refs, lc, tm, even, final):
    if even:
        (x_ref, ma_ref, mb_ref, u_ref, mx_ref, mc_ref, g_ref, wo_ref, w1_ref, w3_ref, w2_ref,
         fin_ref, sd_ref, wglu_ref, o_ref, x1_scr, h_scr, acc_scr) = refs
    else:
        (x_ref, ma_ref, mb_ref, mx_ref, mc_ref, g_ref, wo_ref, w1_ref, w3_ref, w2_ref,
         fin_ref, o_ref, x1_scr, h_scr, acc_scr) = refs
    i = pl.program_id(1)
    j = pl.program_id(2)
    half = ma_ref.shape[2]

    def mod_row(r):
        row = i * tm + lax.broadcasted_iota(jnp.int32, (tm, 1), 0)
        return jnp.where(row < lc, mc_ref[0, r:r + 1, :], mx_ref[0, r:r + 1, :])

    @pl.when(j == 0)
    def _():
        if even:
            z = jax.nn.gelu(sd_ref[...] * u_ref[0] + ma_ref[0])
            zb = z.astype(BF16)
            part_a = (z * _sigmoid(_dot(zb, wglu_ref[...]))).astype(BF16)
        else:
            part_a = ma_ref[0].astype(BF16)
        mixed = (_dot(part_a, wo_ref[0:half, :])
                 + _dot(mb_ref[0].astype(BF16), wo_ref[half:2 * half, :]))
        x1 = x_ref[0] + mod_row(2) * mixed
        x1_scr[...] = x1
        h = _modulated_norm(x1, g_ref[...], mx_ref[0], mc_ref[0], i * tm, lc, 3, 4)
        h_scr[...] = h.astype(BF16)
        acc_scr[...] = jnp.zeros_like(acc_scr)

    h = h_scr[...]
    a = _dot(h, w1_ref[...])
    gate = (a * _sigmoid(a)) * _dot(h, w3_ref[...])
    acc_scr[...] += _dot(gate.astype(BF16), w2_ref[...])

    @pl.when(j == pl.num_programs(2) - 1)
    def _():
        x2 = x1_scr[...] + mod_row(5) * acc_scr[...]
        if final:
            x2 = x2 * lax.rsqrt(jnp.mean(x2 * x2, axis=-1, keepdims=True) + EPS) * fin_ref[...]
        o_ref[0] = x2


def _out_ffn(xa, mix_a, mix_b, mod, g, w_out, w1, w3, w2, fin_g, *, lc, final,
             s5_u=None, s5_d=None, w_glu=None):
    b, s, d = xa.shape
    dff = w1.shape[1]
    half = mix_a.shape[2]
    even = s5_u is not None
    tm = _largest_tile(s, ROW_TILE_TARGET, 16)
    tf = _largest_tile(dff, FF_TILE, LANES)
    row_spec = lambda w: pl.BlockSpec((1, tm, w), lambda bb, i, j: (bb, i, 0))
    const2 = lambda shp: pl.BlockSpec(shp, lambda bb, i, j: (0, 0))
    in_specs = [row_spec(d), row_spec(half), row_spec(half)]
    args = [xa, mix_a, mix_b]
    if even:
        in_specs.append(row_spec(half))
        args.append(s5_u)
    in_specs += [pl.BlockSpec((1, 6, d), lambda bb, i, j: (bb, 0, 0)),
                 pl.BlockSpec((1, 6, d), lambda bb, i, j: (b, 0, 0)),
                 const2((1, d)), const2((2 * half, d)),
                 pl.BlockSpec((d, tf), lambda bb, i, j: (0, j)),
                 pl.BlockSpec((d, tf), lambda bb, i, j: (0, j)),
                 pl.BlockSpec((tf, d), lambda bb, i, j: (j, 0)),
                 const2((1, d))]
    args += [mod, mod, g.reshape(1, d), w_out, w1, w3, w2, fin_g.reshape(1, d)]
    if even:
        in_specs += [const2((1, half)), const2((half, half))]
        args += [s5_d.reshape(1, half), w_glu]
    return pl.pallas_call(
        functools.partial(_out_ffn_kernel, lc=lc, tm=tm, even=even, final=final),
        grid=(b, s // tm, dff // tf),
        in_specs=in_specs,
        out_specs=pl.BlockSpec((1, tm, d), lambda bb, i, j: (bb, i, 0)),
        out_shape=jax.ShapeDtypeStruct((b, s, d), F32),
        scratch_shapes=[pltpu.VMEM((tm, d), F32), pltpu.VMEM((tm, d), BF16),
                        pltpu.VMEM((tm, d), F32)],
        compiler_params=pltpu.CompilerParams(
            dimension_semantics=("parallel", "parallel", "arbitrary")),
    )(*args)


def _rope_angles(l, lc, dim):
    rows = l // GRID_W
    r, col = jnp.meshgrid(jnp.arange(rows, dtype=F32), jnp.arange(GRID_W, dtype=F32), indexing='ij')
    quarter = dim // 4
    inv = ROPE_BASE ** (-jnp.arange(quarter, dtype=F32) / quarter)
    ang = jnp.concatenate([r.reshape(-1, 1) * inv, col.reshape(-1, 1) * inv], axis=-1)
    cos = jnp.concatenate([jnp.ones((lc, dim // 2), F32), jnp.cos(ang)], axis=0)
    sin = jnp.concatenate([jnp.zeros((lc, dim // 2), F32), jnp.sin(ang)], axis=0)
    return cos, sin


def _split_pairs_perm(n):
    return np.concatenate([np.arange(0, n, 2), np.arange(1, n, 2)])


def _s5_matrices(a_re, a_im, log_dt, b_re, b_im, c_re, c_im):
    t_len = S5_T
    g, p = a_re.shape[1], a_re.shape[2]
    k = b_re.shape[3]
    tau = jnp.arange(t_len + 1, dtype=F32)
    toep = jnp.zeros((g, t_len, k, t_len, k), F32)
    f_parts, e_parts, a_chunk = [], [], []
    tt = jnp.arange(t_len)
    for r in range(2):
        dt = jnp.exp(log_dt[r])[:, None]
        mag = jnp.exp(a_re[r] * dt)
        ab_re = mag * jnp.cos(a_im[r] * dt)
        ab_im = mag * jnp.sin(a_im[r] * dt)
        nr, ni = ab_re - 1.0, ab_im
        den = a_re[r] * a_re[r] + a_im[r] * a_im[r]
        fr = (nr * a_re[r] + ni * a_im[r]) / den
        fi = (ni * a_re[r] - nr * a_im[r]) / den
        bb_re = fr[..., None] * b_re[r] - fi[..., None] * b_im[r]
        bb_im = fr[..., None] * b_im[r] + fi[..., None] * b_re[r]
        pw_mag = jnp.exp(tau[:, None, None] * (a_re[r] * dt)[None])
        pw_ang = tau[:, None, None] * (a_im[r] * dt)[None]
        pw_re = pw_mag * jnp.cos(pw_ang)
        pw_im = pw_mag * jnp.sin(pw_ang)
        ca_re = c_re[r][None] * pw_re[:, :, None, :] - c_im[r][None] * pw_im[:, :, None, :]
        ca_im = c_re[r][None] * pw_im[:, :, None, :] + c_im[r][None] * pw_re[:, :, None, :]
        m = (jnp.einsum('tgop,gpk->tgok', ca_re, bb_re, precision=lax.Precision.HIGHEST)
             - jnp.einsum('tgop,gpk->tgok', ca_im, bb_im, precision=lax.Precision.HIGHEST))
        lag = (tt[None, :] - tt[:, None]) if r == 0 else (tt[:, None] - tt[None, :])
        valid = lag >= 0
        mg = m[jnp.clip(lag, 0, t_len - 1)]
        mg = jnp.where(valid[:, :, None, None, None], mg, 0.0)
        toep = toep + mg.transpose(2, 0, 4, 1, 3)
        pidx = (t_len - 1 - tt) if r == 0 else tt
        f_re = pw_re[pidx][:, :, :, None] * bb_re[None] - pw_im[pidx][:, :, :, None] * bb_im[None]
        f_im = pw_re[pidx][:, :, :, None] * bb_im[None] + pw_im[pidx][:, :, :, None] * bb_re[None]
        f_parts += [f_re.transpose(1, 0, 3, 2).reshape(g, t_len * k, p),
                    f_im.transpose(1, 0, 3, 2).reshape(g, t_len * k, p)]
        qidx = (tt + 1) if r == 0 else (t_len - tt)
        e_parts += [ca_re[qidx].transpose(1, 3, 0, 2).reshape(g, p, t_len * k),
                    (-ca_im[qidx]).transpose(1, 3, 0, 2).reshape(g, p, t_len * k)]
        a_chunk += [pw_re[t_len].reshape(-1), pw_im[t_len].reshape(-1)]
    toep = toep.reshape(g, t_len * k, t_len * k).astype(BF16)
    pad_lo = lambda m_: jnp.concatenate([m_, jnp.zeros_like(m_)], axis=-1)
    pad_hi = lambda m_: jnp.concatenate([jnp.zeros_like(m_), m_], axis=-1)
    odd = (jnp.arange(g) % 2 == 1)[:, None, None]
    f_pad = jnp.stack([jnp.where(odd, pad_hi(m_), pad_lo(m_)) for m_ in f_parts], axis=1)
    pad_lo_r = lambda m_: jnp.concatenate([m_, jnp.zeros_like(m_)], axis=-2)
    pad_hi_r = lambda m_: jnp.concatenate([jnp.zeros_like(m_), m_], axis=-2)
    e_pad = jnp.stack([jnp.where(odd, pad_hi_r(m_), pad_lo_r(m_)) for m_ in e_parts], axis=1)
    return toep, f_pad.astype(BF16), e_pad.astype(BF16), jnp.stack(a_chunk, axis=0)


def _s5_inject_kernel(u_ref, f_ref, o0, o1, o2, o3):
    u0 = u_ref[0]
    u1 = u_ref[1]
    for a, o in enumerate((o0, o1, o2, o3)):
        o[...] = _dot(u0, f_ref[0, a]) + _dot(u1, f_ref[1, a])


def _s5_scan_kernel(s0, s1, s2, s3, a_ref, h0, h1, h2, h3, *, nb, nc, ncc):
    a = a_ref[...]
    arf, aif, arb, aib = a[0:1], a[1:2], a[2:3], a[3:4]
    ct = s0.shape[1]

    def body(i, carry):
        cb = jnp.where(i < ncc, ncc - 1 - i, nc - 1 - (i - ncc))
        new = []
        for bb in range(nb):
            hr, hi, gr, gi = carry[4 * bb:4 * bb + 4]
            rf = bb * nc + i
            rb = bb * nc + cb
            h0[pl.ds(rf, 1), :] = hr
            h1[pl.ds(rf, 1), :] = hi
            h2[pl.ds(rb, 1), :] = gr
            h3[pl.ds(rb, 1), :] = gi
            new += [arf * hr - aif * hi + s0[pl.ds(rf, 1), :],
                    arf * hi + aif * hr + s1[pl.ds(rf, 1), :],
                    arb * gr - aib * gi + s2[pl.ds(rb, 1), :],
                    arb * gi + aib * gr + s3[pl.ds(rb, 1), :]]
        return tuple(new)

    zero = jnp.zeros((1, ct), F32)
    lax.fori_loop(0, nc, body, (zero,) * (4 * nb))


def _s5_readout_kernel(u_ref, t_ref, h0, h1, h2, h3, e_ref, o_ref):
    y = _dot(u_ref[0], t_ref[0])
    for a, h in enumerate((h0, h1, h2, h3)):
        y = y + _dot(h[...].astype(BF16), e_ref[0, a])
    o_ref[0] = y


def _s5_mixer(proj, mats, *, lc):
    toep, f_pad, e_pad, a_chunk = mats
    b, s, _ = proj.shape
    g = toep.shape[0]
    k = S5_GROUP
    width = g * k
    nc = s // S5_T
    ncc = lc // S5_T
    rows = b * nc
    tk = S5_T * k
    u = proj[:, :, :width].reshape(b, nc, S5_T, g, k).transpose(3, 0, 1, 2, 4)
    u = u.reshape(g, rows, tk).astype(BF16)
    sw = g * S5_STATE
    st_shape = jax.ShapeDtypeStruct((rows, sw), F32)
    inj = pl.pallas_call(
        _s5_inject_kernel,
        grid=(g // 2,),
        in_specs=[pl.BlockSpec((2, rows, tk), lambda j: (j, 0, 0)),
                  pl.BlockSpec((2, 4, tk, LANES), lambda j: (j, 0, 0, 0))],
        out_specs=[pl.BlockSpec((rows, LANES), lambda j: (0, j))] * 4,
        out_shape=[st_shape] * 4,
    )(u, f_pad)
    ct = _largest_tile(sw, 512, LANES)
    states = pl.pallas_call(
        functools.partial(_s5_scan_kernel, nb=b, nc=nc, ncc=ncc),
        grid=(sw // ct,),
        in_specs=[pl.BlockSpec((rows, ct), lambda j: (0, j))] * 4
        + [pl.BlockSpec((4, ct), lambda j: (0, j))],
        out_specs=[pl.BlockSpec((rows, ct), lambda j: (0, j))] * 4,
        out_shape=[st_shape] * 4,
    )(*inj, a_chunk)
    y = pl.pallas_call(
        _s5_readout_kernel,
        grid=(g,),
        in_specs=[pl.BlockSpec((1, rows, tk), lambda j: (j, 0, 0)),
                  pl.BlockSpec((1, tk, tk), lambda j: (j, 0, 0))]
        + [pl.BlockSpec((rows, LANES), lambda j: (0, j // 2))] * 4
        + [pl.BlockSpec((1, 4, LANES, tk), lambda j: (j, 0, 0, 0))],
        out_specs=pl.BlockSpec((1, rows, tk), lambda j: (j, 0, 0)),
        out_shape=jax.ShapeDtypeStruct((g, rows, tk), F32),
    )(u, toep, *states, e_pad)
    y = y.reshape(g, b, nc, S5_T, k).transpose(1, 2, 3, 0, 4)
    return y.reshape(b, s, width)


def _mla_weights(w_uq, w_ukv):
    rq = w_uq.shape[0]
    rkv = w_ukv.shape[0]
    hq = w_uq.reshape(rq, MLA_HEADS, MLA_NOPE + MLA_ROPE)
    nope = hq[:, :, :MLA_NOPE]
    r_even = hq[:, :, MLA_NOPE::2]
    r_odd = hq[:, :, MLA_NOPE + 1::2]
    zq = jnp.zeros((rq, MLA_HEADS, HEAD_PAD - MLA_NOPE - MLA_ROPE), F32)
    wq = jnp.concatenate([nope, r_even, r_odd, zq], axis=-1)
    wq_sw = jnp.concatenate([jnp.zeros_like(nope), r_odd, r_even, zq], axis=-1)
    hkv = w_ukv.reshape(rkv, MLA_HEADS, MLA_NOPE + MLA_V)
    zk = jnp.zeros((rkv, MLA_HEADS, HEAD_PAD - MLA_NOPE), F32)
    wk = jnp.concatenate([hkv[:, :, :MLA_NOPE], zk], axis=-1)
    wv = jnp.concatenate([hkv[:, :, MLA_NOPE:], jnp.zeros((rkv, MLA_HEADS, HEAD_PAD - MLA_V), F32)],
                         axis=-1)
    out_in = lambda w: w.transpose(1, 2, 0).astype(BF16)
    return out_in(wq), out_in(wq_sw), wk.transpose(1, 0, 2).astype(BF16), out_in(wv)


def _mla_tables(l, lc):
    cos, sin = _rope_angles(l, lc, MLA_ROPE)
    s = cos.shape[0]
    z_lo = jnp.zeros((s, MLA_NOPE), F32)
    z_hi = jnp.zeros((s, HEAD_PAD - MLA_NOPE - MLA_ROPE), F32)
    ck = jnp.concatenate([z_lo, cos, cos, z_hi], axis=-1)
    sk = jnp.concatenate([z_lo, -sin, sin, z_hi], axis=-1)
    scale = (MLA_NOPE + MLA_ROPE) ** -0.5 * math.log2(math.e)
    cq = jnp.concatenate([jnp.ones_like(z_lo), cos, cos, z_hi], axis=-1) * scale
    return cq.T, sk.T * scale, ck, sk


def _mla_proj_kernel(cq_ref, ckv_ref, kr_ref, krs_ref, qn_ref, kvn_ref, wqt_ref, wqst_ref, wk_ref,
                     wvt_ref, tcq_ref, tsq_ref, tck_ref, tsk_ref, qt_ref, k_ref, vt_ref):
    def norm(v, g):
        return (v * lax.rsqrt(jnp.mean(v * v, axis=-1, keepdims=True) + EPS) * g).astype(BF16)

    hq = norm(cq_ref[0], qn_ref[...])
    hkv = norm(ckv_ref[0], kvn_ref[...])
    k_rope = kr_ref[0] * tck_ref[...] + krs_ref[0] * tsk_ref[...]
    row = lax.broadcasted_iota(jnp.int32, (HEAD_PAD, 1), 0)
    ones_row = (row == MLA_V).astype(F32)
    tcq = tcq_ref[...]
    tsq = tsq_ref[...]
    for hh in range(MLA_HEADS):
        qt = _dot_nt(wqt_ref[hh], hq) * tcq + _dot_nt(wqst_ref[hh], hq) * tsq
        qt_ref[0, hh] = qt.astype(BF16)
        k_ref[0, hh] = (_dot(hkv, wk_ref[hh]) + k_rope).astype(BF16)
        vt_ref[0, hh, 0] = (_dot_nt(wvt_ref[hh], hkv) + ones_row).astype(BF16)


def _attn_kernel(qt_ref, k_ref, vt_ref, o_ref, st_scr, acc_scr, *, lc, tq, tk, s_total):
    heads = qt_ref.shape[1]
    kc = ATT_SCORE_CHUNK
    pc = ATT_PROB_CHUNK
    sub = 8

    def score_chunk(slot, hh, first, c, part):
        row0 = first + c * kc
        if not isinstance(row0, int):
            row0 = pl.multiple_of(row0, kc)
        st = _dot(k_ref[0, hh, pl.ds(row0, kc), :], qt_ref[0, hh])
        st_scr[slot, hh, c * kc:(c + 1) * kc, :] = st
        cm = jnp.max(st.reshape(kc // sub, sub, tq), axis=0)
        return cm if part is None else jnp.maximum(part, cm)

    def prob_chunk(slot, hh, tile, j, m, alpha):
        p = jnp.exp2(st_scr[slot, hh, j * pc:(j + 1) * pc, :] - m).astype(BF16)
        pv = _dot(vt_ref[0, hh, tile, :, j * pc:(j + 1) * pc], p)
        if j == 0:
            acc_scr[hh] = acc_scr[hh] * alpha + pv
        else:
            acc_scr[hh] += pv

    def fold_max(m_old, part):
        m_new = jnp.maximum(m_old, jnp.max(part, axis=0, keepdims=True))
        return m_new, jnp.exp2(m_old - m_new)

    def phase(nkeys, a_args, bc_args):
        parts = [None] * heads
        assert pc // kc == heads and nkeys % pc == 0
        for j in range(nkeys // pc):
            for r in range(heads):
                if a_args is not None:
                    for hh in range(heads):
                        parts[hh] = score_chunk(a_args[0], hh, a_args[1], j * heads + r, parts[hh])
                if bc_args is not None:
                    slot, tile, ms, alphas = bc_args
                    prob_chunk(slot, r, tile, j, ms[r], alphas[r])
        return parts

    def finish():
        out_t = jnp.concatenate(
            [acc_scr[hh, 0:MLA_V, :] / acc_scr[hh, MLA_V:MLA_V + 1, :] for hh in range(heads)],
            axis=0)
        o_ref[0] = out_t.T

    def attend(nkeys, n_tiles):
        acc_scr[...] = jnp.zeros(acc_scr.shape, F32)
        neg = jnp.full((1, tq), -jnp.inf, F32)
        parts = phase(nkeys, (0, 0), None)
        state = []
        for hh in range(heads):
            state += list(fold_max(neg, parts[hh]))

        def step(t, slot, carry):
            ms, alphas = carry[0::2], carry[1::2]
            nxt = phase(nkeys, (1 - slot, (t + 1) * nkeys), (slot, t, ms, alphas))
            new = []
            for hh in range(heads):
                new += list(fold_max(ms[hh], nxt[hh]))
            return tuple(new)

        pairs = (n_tiles - 1) // 2
        state = tuple(state)
        if pairs > 0:
            state = lax.fori_loop(
                0, pairs, lambda i, carry: step(2 * i + 1, 1, step(2 * i, 0, carry)), state)
        for t in range(2 * pairs, n_tiles - 1):
            state = step(t, t % 2, state)
        phase(nkeys, None, ((n_tiles - 1) % 2, n_tiles - 1, state[0::2], state[1::2]))
        finish()

    is_ctx = pl.program_id(2) < lc // tq

    @pl.when(is_ctx)
    def _():
        attend(lc, 1)

    @pl.when(jnp.logical_not(is_ctx))
    def _():
        attend(tk, s_total // tk)


def _mla_mixer(proj, q_norm, kv_norm, weights, tables, *, lc):
    b, s, _ = proj.shape
    wqt, wqst, wk, wvt = weights
    tq = ATT_TQ
    tk = _largest_tile(s, ATT_TK, ATT_PROB_CHUNK)
    assert lc % tq == 0 and s % tq == 0 and lc <= tk and lc % ATT_PROB_CHUNK == 0
    rq = wqt.shape[2]
    rkv = wk.shape[1]
    base = 512
    head_shape = jax.ShapeDtypeStruct((b, MLA_HEADS, s, HEAD_PAD), BF16)
    full = lambda shp: pl.BlockSpec(shp, lambda bb, i: tuple(0 for _ in shp))
    tab = pl.BlockSpec((tk, HEAD_PAD), lambda bb, i: (i, 0))
    tab_t = pl.BlockSpec((HEAD_PAD, tk), lambda bb, i: (0, i))
    head_rows = pl.BlockSpec((1, MLA_HEADS, tk, HEAD_PAD), lambda bb, i: (bb, 0, i, 0))
    qt, k, vt = pl.pallas_call(
        _mla_proj_kernel,
        grid=(b, s // tk),
        in_specs=[pl.BlockSpec((1, tk, rq), lambda bb, i: (bb, i, base // rq)),
                  pl.BlockSpec((1, tk, rkv), lambda bb, i: (bb, i, (base + rq) // rkv)),
                  pl.BlockSpec((1, tk, LANES), lambda bb, i: (bb, i, (base + rq + rkv) // LANES)),
                  pl.BlockSpec((1, tk, LANES), lambda bb, i: (bb, i, (base + rq + rkv) // LANES + 1)),
                  full((1, rq)), full((1, rkv)),
                  full(wqt.shape), full(wqst.shape), full(wk.shape), full(wvt.shape),
                  tab_t, tab_t, tab, tab],
        out_specs=[pl.BlockSpec((1, MLA_HEADS, HEAD_PAD, tk), lambda bb, i: (bb, 0, 0, i)), head_rows,
                   pl.BlockSpec((1, MLA_HEADS, 1, HEAD_PAD, tk), lambda bb, i: (bb, 0, i, 0, 0))],
        out_shape=[jax.ShapeDtypeStruct((b, MLA_HEADS, HEAD_PAD, s), BF16), head_shape,
                   jax.ShapeDtypeStruct((b, MLA_HEADS, s // tk, HEAD_PAD, tk), BF16)],
    )(proj, proj, proj, proj, q_norm.reshape(1, rq), kv_norm.reshape(1, rkv),
      wqt, wqst, wk, wvt, *tables)
    return pl.pallas_call(
        functools.partial(_attn_kernel, lc=lc, tq=tq, tk=tk, s_total=s),
        grid=(b, MLA_HEADS // 2, s // tq),
        in_specs=[pl.BlockSpec((1, 2, HEAD_PAD, tq), lambda bb, hp, i: (bb, hp, 0, i)),
                  pl.BlockSpec((1, 2, s, HEAD_PAD), lambda bb, hp, i: (bb, hp, 0, 0)),
                  pl.BlockSpec((1, 2, s // tk, HEAD_PAD, tk), lambda bb, hp, i: (bb, hp, 0, 0, 0))],
        out_specs=pl.BlockSpec((1, tq, 2 * MLA_V), lambda bb, hp, i: (bb, i, hp)),
        out_shape=jax.ShapeDtypeStruct((b, s, MLA_HEADS * MLA_V), F32),
        scratch_shapes=[pltpu.VMEM((2, 2, tk, tq), F32), pltpu.VMEM((2, HEAD_PAD, tq), F32)],
        compiler_params=pltpu.CompilerParams(
            dimension_semantics=("parallel", "parallel", "arbitrary")),
    )(qt, k, vt)


def _state_scan_kernel(kv_ref, dec_ref, st_ref, *, nch, ncc):
    hd = HEAD_DIM

    def body(i, carry):
        sf, sb = carry
        cb = jnp.where(i < ncc, ncc - 1 - i, nch - 1 - (i - ncc))
        st_ref[0, 0, i, :, 0:hd] = sf.astype(BF16)
        st_ref[0, 0, cb, :, hd:2 * hd] = sb.astype(BF16)
        sf = dec_ref[0, 0, i, :, 0:hd] * sf + kv_ref[0, 0, i, :, 0:hd]
        sb = dec_ref[0, 0, cb, :, hd:2 * hd] * sb + kv_ref[0, 0, cb, :, hd:2 * hd]
        return sf, sb

    zero = jnp.zeros((hd, hd), F32)
    lax.fori_loop(0, nch, body, (zero, zero))


def _state_scan(kv, dec, *, ncc):
    b, h, nch = kv.shape[:3]
    hd = HEAD_DIM
    return pl.pallas_call(
        functools.partial(_state_scan_kernel, nch=nch, ncc=ncc),
        grid=(b, h),
        in_specs=[pl.BlockSpec((1, 1, nch, hd, 2 * hd), lambda bb, hh: (bb, hh, 0, 0, 0)),
                  pl.BlockSpec((1, 1, nch, 1, 2 * hd), lambda bb, hh: (bb, hh, 0, 0, 0))],
        out_specs=pl.BlockSpec((1, 1, nch, hd, 2 * hd), lambda bb, hh: (bb, hh, 0, 0, 0)),
        out_shape=jax.ShapeDtypeStruct((b, h, nch, hd, 2 * hd), BF16),
    )(kv, dec)


def _readout_kernel(oi_ref, qd_ref, st_ref, g_ref, gn_ref, o_ref, *, cpt, center):
    def body(c, _):
        r0 = pl.multiple_of(c * CHUNK, CHUNK)
        o = oi_ref[0, 0, pl.ds(r0, CHUNK), :] + _dot_nt(qd_ref[0, 0, pl.ds(r0, CHUNK), :],
                                                        st_ref[0, 0, c])
        if center:
            o = o - jnp.mean(o, axis=-1, keepdims=True)
        o = o * lax.rsqrt(jnp.mean(o * o, axis=-1, keepdims=True) + EPS) * gn_ref[...]
        gate = g_ref[0, pl.ds(r0, CHUNK), :]
        o_ref[0, pl.ds(r0, CHUNK), :] = o * (gate * _sigmoid(gate))
        return 0

    lax.fori_loop(0, cpt, body, 0, unroll=True)


def _readout(o_intra, q_dec, states, proj, gate_col, gn, *, center):
    b, h, s, hd = o_intra.shape
    nch = s // CHUNK
    cpt = _largest_tile(nch, 11, 1)
    tl = cpt * CHUNK
    return pl.pallas_call(
        functools.partial(_readout_kernel, cpt=cpt, center=center),
        grid=(b, h, nch // cpt),
        in_specs=[pl.BlockSpec((1, 1, tl, hd), lambda bb, hh, t: (bb, hh, t, 0)),
                  pl.BlockSpec((1, 1, tl, 2 * hd), lambda bb, hh, t: (bb, hh, t, 0)),
                  pl.BlockSpec((1, 1, cpt, hd, 2 * hd), lambda bb, hh, t: (bb, hh, t, 0, 0)),
                  pl.BlockSpec((1, tl, hd), lambda bb, hh, t: (bb, t, gate_col + hh)),
                  pl.BlockSpec((1, hd), lambda bb, hh, t: (0, hh))],
        out_specs=pl.BlockSpec((1, tl, hd), lambda bb, hh, t: (bb, t, hh)),
        out_shape=jax.ShapeDtypeStruct((b, s, h * hd), F32),
    )(o_intra, q_dec, states, proj, gn.reshape(1, h * hd))


def _local_out_shapes(b, h, s):
    nch = s // CHUNK
    hd = HEAD_DIM
    return [jax.ShapeDtypeStruct((b, h, s, hd), F32),
            jax.ShapeDtypeStruct((b, h, s, 2 * hd), BF16),
            jax.ShapeDtypeStruct((b, h, nch, hd, 2 * hd), F32)]


def _local_out_specs(cpt):
    tl = cpt * CHUNK
    hd = HEAD_DIM
    return [pl.BlockSpec((1, 1, tl, hd), lambda bb, hh, t: (bb, hh, t, 0)),
            pl.BlockSpec((1, 1, tl, 2 * hd), lambda bb, hh, t: (bb, hh, t, 0)),
            pl.BlockSpec((1, 1, cpt, hd, 2 * hd), lambda bb, hh, t: (bb, hh, t, 0, 0))]


def _ret_constants():
    c = CHUNK
    pos = np.arange(c, dtype=np.float64)
    rel = pos[:, None] - pos[None, :]
    scale = HEAD_DIM ** -0.5
    dmask = np.zeros((RET_HEADS, c, c))
    qw = np.zeros((RET_HEADS, c, 2 * HEAD_DIM))
    kw = np.zeros((RET_HEADS, c, 2 * HEAD_DIM))
    dec = np.zeros((RET_HEADS, 1, 2 * HEAD_DIM))
    for r, offset in enumerate(RET_DECAY_OFFSETS):
        lg = np.log1p(-np.exp2(-(5.0 + offset) - np.arange(RET_HEADS, dtype=np.float64)))[:, None, None]
        dist = rel if r == 0 else -rel
        dmask += np.where(dist >= 0, np.exp(lg * np.maximum(dist, 0.0)), 0.0) * scale
        steps_in = (pos + 1) if r == 0 else (c - pos)
        steps_out = (c - 1 - pos) if r == 0 else pos
        sl = slice(r * HEAD_DIM, (r + 1) * HEAD_DIM)
        qw[:, :, sl] = np.exp(lg[:, :, 0] * steps_in[None, :])[:, :, None]
        kw[:, :, sl] = np.exp(lg[:, :, 0] * steps_out[None, :])[:, :, None] * scale
        dec[:, :, sl] = np.exp(lg * c)
    return (jnp.asarray(dmask, F32), jnp.asarray(qw, F32), jnp.asarray(kw, F32), jnp.asarray(dec, F32))


def _ret_local_kernel(q_ref, k_ref, v_ref, dm_ref, qw_ref, kw_ref, oi_ref, qd_ref, kv_ref, *, cpt):
    dm = dm_ref[0]
    qw = qw_ref[0]
    kw = kw_ref[0]

    def body(c, _):
        r0 = pl.multiple_of(c * CHUNK, CHUNK)
        q = q_ref[0, pl.ds(r0, CHUNK), :]
        k = k_ref[0, pl.ds(r0, CHUNK), :]
        v = v_ref[0, pl.ds(r0, CHUNK), :].astype(BF16)
        inner = _dot_nt(q.astype(BF16), k.astype(BF16)) * dm
        oi_ref[0, 0, pl.ds(r0, CHUNK), :] = _dot(inner.astype(BF16), v)
        q2 = jnp.concatenate([q, q], axis=-1) * qw
        qd_ref[0, 0, pl.ds(r0, CHUNK), :] = q2.astype(BF16)
        k2 = (jnp.concatenate([k, k], axis=-1) * kw).astype(BF16)
        kv_ref[0, 0, c] = _dot_tn(v, k2)
        return 0

    lax.fori_loop(0, cpt, body, 0, unroll=True)


def _retention_mixer(proj, gn, consts, *, lc):
    b, s, _ = proj.shape
    h = RET_HEADS
    hd = HEAD_DIM
    dmask, qw, kw, dec = consts
    nch = s // CHUNK
    cpt = _largest_tile(nch, 11, 1)
    tl = cpt * CHUNK
    col = lambda base: pl.BlockSpec((1, tl, hd), lambda bb, hh, t: (bb, t, base + hh))
    per_head = lambda w: pl.BlockSpec((1, CHUNK, w), lambda bb, hh, t: (hh, 0, 0))
    o_intra, q_dec, kv = pl.pallas_call(
        functools.partial(_ret_local_kernel, cpt=cpt),
        grid=(b, h, nch // cpt),
        in_specs=[col(0), col(h), col(2 * h), per_head(CHUNK), per_head(2 * hd), per_head(2 * hd)],
        out_specs=_local_out_specs(cpt),
        out_shape=_local_out_shapes(b, h, s),
    )(proj, proj, proj, dmask, qw, kw)
    dec_all = jnp.broadcast_to(dec[None, :, None], (b, h, nch, 1, 2 * hd))
    states = _state_scan(kv, dec_all, ncc=lc // CHUNK)
    return _readout(o_intra, q_dec, states, proj, 3 * h, gn, center=True)


HG_LEVELS = (1, 2, 4, 8, 16, 32, 64)


def _hg_constants():
    c = CHUNK
    t = np.arange(c)[:, None]
    s = np.arange(c)[None, :]
    sums, masks = [], []
    for blk in HG_LEVELS:
        same = (t // blk) == (s // blk)
        right = (t // blk) % 2 == 1
        sums.append((right & same & (s <= t)) | (~right & same & (s > t)))
        masks.append(((t // (2 * blk)) == (s // (2 * blk))) & right & ((s // blk) % 2 == 0))
    sums.append(np.broadcast_to(s <= t, (c, c)))
    sums.append(np.broadcast_to(s > t, (c, c)))
    sums.append(((t // HG_BLOCK) == (s // HG_BLOCK)) & (s <= t))
    g_f = np.concatenate([m.astype(np.float32) for m in sums], axis=0)
    m_f = np.stack([m.astype(np.float32) for m in masks], axis=0)
    flip = lambda a: a[..., ::-1, ::-1]
    g_b = np.concatenate([flip(m.astype(np.float32)) for m in sums], axis=0)
    m_b = np.stack([flip(m.astype(np.float32)) for m in masks], axis=0)
    return (jnp.asarray(np.stack([g_f, g_b]), BF16), jnp.asarray(np.stack([m_f, m_b]), F32))


def _hg_local_kernel(q_ref, ff_ref, fb_ref, v_ref, lb_ref, gs_ref, mk_ref, oi_ref, qd_ref, kv_ref,
                     dec_ref, *, cpt):
    c_len = CHUNK
    nlev = len(HG_LEVELS)
    lb = lb_ref[...]
    row = lax.broadcasted_iota(jnp.int32, (c_len, c_len), 0)
    col = lax.broadcasted_iota(jnp.int32, (c_len, c_len), 1)
    row_in_blk = row % HG_BLOCK

    def body(c, _):
        r0 = pl.multiple_of(c * c_len, c_len)
        q = q_ref[0, pl.ds(r0, c_len), :]
        v = v_ref[0, pl.ds(r0, c_len), :].astype(BF16)
        att = jnp.zeros((c_len, c_len), F32)
        q_parts, kv_parts, dec_parts = [], [], []
        for d, f_ref in enumerate((ff_ref, fb_ref)):
            f = lb + (1.0 - lb) * _sigmoid(f_ref[0, pl.ds(r0, c_len), :])
            kk = 1.0 - f
            z = _dot(gs_ref[d], jnp.log(f).astype(BF16))
            for lv in range(nlev):
                e = jnp.exp(z[lv * c_len:(lv + 1) * c_len])
                att = att + mk_ref[d, lv] * _dot_nt((q * e).astype(BF16), (kk * e).astype(BF16))
            zq = z[nlev * c_len:(nlev + 1) * c_len]
            zk = z[(nlev + 1) * c_len:(nlev + 2) * c_len]
            cl = z[(nlev + 2) * c_len:(nlev + 3) * c_len]
            for delta in range(HG_BLOCK):
                if delta == 0:
                    prod = q * kk
                else:
                    sh = delta if d == 0 else c_len - delta
                    prod = (q * pltpu.roll(kk, sh, 0)
                            * jnp.exp(jnp.minimum(cl - pltpu.roll(cl, sh, 0), 0.0)))
                a = jnp.sum(prod, axis=-1, keepdims=True)
                if d == 0:
                    hit = (col == row - delta) & (row_in_blk >= delta)
                else:
                    hit = (col == row + delta) & (row_in_blk + delta < HG_BLOCK)
                att = att + jnp.where(hit, a, 0.0)
            q_parts.append(q * jnp.exp(zq))
            kv_parts.append(kk * jnp.exp(zk))
            last = c_len - 1 if d == 0 else 0
            dec_parts.append(jnp.exp(zq[last:last + 1]))
        oi_ref[0, 0, pl.ds(r0, c_len), :] = _dot(att.astype(BF16), v)
        qd_ref[0, 0, pl.ds(r0, c_len), :] = jnp.concatenate(q_parts, axis=-1).astype(BF16)
        kv_ref[0, 0, c] = _dot_tn(v, jnp.concatenate(kv_parts, axis=-1).astype(BF16))
        dec_ref[0, 0, c] = jnp.concatenate(dec_parts, axis=-1)
        return 0

    lax.fori_loop(0, cpt, body, 0, unroll=2)


def _hgrn2_mixer(proj, lb, gn, consts, *, lc, col0):
    b, s, _ = proj.shape
    h = HG_HEADS
    hd = HEAD_DIM
    gsum, masks = consts
    nch = s // CHUNK
    cpt = _largest_tile(nch, 11, 1)
    tl = cpt * CHUNK
    col = lambda base: pl.BlockSpec((1, tl, hd), lambda bb, hh, t: (bb, t, col0 + base + hh))
    full = lambda a: pl.BlockSpec(a.shape, lambda bb, hh, t: tuple(0 for _ in a.shape))
    o_intra, q_dec, kv, dec = pl.pallas_call(
        functools.partial(_hg_local_kernel, cpt=cpt),
        grid=(b, h, nch // cpt),
        in_specs=[col(0), col(h), col(2 * h), col(3 * h),
                  pl.BlockSpec((1, hd), lambda bb, hh, t: (0, hh)), full(gsum), full(masks)],
        out_specs=_local_out_specs(cpt)
        + [pl.BlockSpec((1, 1, cpt, 1, 2 * hd), lambda bb, hh, t: (bb, hh, t, 0, 0))],
        out_shape=_local_out_shapes(b, h, s)
        + [jax.ShapeDtypeStruct((b, h, nch, 1, 2 * hd), F32)],
    )(proj, proj, proj, proj, lb.reshape(1, h * hd), gsum, masks)
    states = _state_scan(kv, dec, ncc=lc // CHUNK)
    return _readout(o_intra, q_dec, states, proj, col0 + 4 * h, gn, center=False)


def kernel(x, c, ctx, c_ctx, w_mod, b_mod, norm1_g, norm2_g, ffn_w1, ffn_w3, ffn_w2, w_in_even, w_out_even, s5_a_re, s5_a_im, s5_log_dt, s5_b_re, s5_b_im, s5_c_re, s5_c_im, s5_d, s5_w_glu, mla_q_norm, mla_w_uq, mla_kv_norm, mla_w_ukv, w_in_odd, w_out_odd, ret_gn, hg_lb_logits, hg_gn, final_norm):
    b, l, d = x.shape
    lc = ctx.shape[1]
    depth = w_mod.shape[0]
    n_odd = w_in_odd.shape[0]
    assert lc % CHUNK == 0 and l % CHUNK == 0 and l % GRID_W == 0

    xa = jnp.concatenate([ctx, x], axis=1)
    mods = _modulation(c, c_ctx, w_mod, b_mod)

    cos_r, sin_r = _rope_angles(l, lc, HEAD_DIM)
    ret_cos = jnp.concatenate([cos_r, cos_r], axis=-1)
    ret_sin = jnp.concatenate([-sin_r, sin_r], axis=-1)
    mla_tabs = _mla_tables(l, lc)
    ret_consts = _ret_constants()
    hg_consts = _hg_constants()
    hg_lb = jnp.cumsum(jax.nn.softmax(hg_lb_logits.astype(F32), axis=0), axis=0)[:n_odd]

    s5_w = s5_d.shape[1]
    kr0 = w_in_even.shape[2] - MLA_ROPE
    kr_e = w_in_even[:, :, kr0::2]
    kr_o = w_in_even[:, :, kr0 + 1::2]
    z_lo = jnp.zeros(w_in_even.shape[:2] + (MLA_NOPE,), F32)
    z_hi = jnp.zeros(w_in_even.shape[:2] + (HEAD_PAD - MLA_NOPE - MLA_ROPE,), F32)
    w_in_e = jnp.concatenate([w_in_even[:, :, :kr0], z_lo, kr_e, kr_o, z_hi, z_lo, kr_o, kr_e, z_hi],
                             axis=-1).astype(BF16)
    qk_w = 2 * RET_HEADS * HEAD_DIM
    perm = (np.arange(qk_w) // HEAD_DIM) * HEAD_DIM + np.tile(_split_pairs_perm(HEAD_DIM),
                                                               qk_w // HEAD_DIM)
    w_in_o = jnp.concatenate([w_in_odd[:, :, perm], w_in_odd[:, :, qk_w:]], axis=-1).astype(BF16)

    s5_mats = jax.vmap(_s5_matrices)(s5_a_re, s5_a_im, s5_log_dt, s5_b_re, s5_b_im, s5_c_re, s5_c_im)
    mla_w = jax.vmap(_mla_weights)(mla_w_uq, mla_w_ukv)

    w1 = ffn_w1.astype(BF16)
    w3 = ffn_w3.astype(BF16)
    w2 = ffn_w2.astype(BF16)
    w_out_e = w_out_even.astype(BF16)
    w_out_o = w_out_odd.astype(BF16)
    w_glu = s5_w_glu.astype(BF16)

    for li in range(depth):
        j = li // 2
        final = li == depth - 1
        if li % 2 == 0:
            proj = _in_proj(xa, mods[li], norm1_g[li], w_in_e[j], ret_cos, ret_sin,
                            lc=lc, tn=w_in_e.shape[2], n_rope=0)
            y_s5 = _s5_mixer(proj, tuple(m[j] for m in s5_mats), lc=lc)
            attn = _mla_mixer(proj, mla_q_norm[j], mla_kv_norm[j],
                              tuple(w[j] for w in mla_w), mla_tabs, lc=lc)
            xa = _out_ffn(xa, y_s5, attn, mods[li], norm2_g[li], w_out_e[j], w1[li], w3[li], w2[li],
                          final_norm, lc=lc, final=final, s5_u=proj, s5_d=s5_d[j], w_glu=w_glu[j])
        else:
            proj = _in_proj(xa, mods[li], norm1_g[li], w_in_o[j], ret_cos, ret_sin,
                            lc=lc, tn=RET_HEADS * HEAD_DIM, n_rope=2)
            ret = _retention_mixer(proj, ret_gn[j], ret_consts, lc=lc)
            hg = _hgrn2_mixer(proj, hg_lb[j], hg_gn[j], hg_consts, lc=lc, col0=4 * RET_HEADS)
            xa = _out_ffn(xa, ret, hg, mods[li], norm2_g[li], w_out_o[j], w1[li], w3[li], w2[li],
                          final_norm, lc=lc, final=final)
    return xa[:, lc:, :]
```

```python
import functools
import math

import jax
import jax.numpy as jnp
import numpy as np
from jax import lax
from jax.experimental import pallas as pl
from jax.experimental.pallas import tpu as pltpu

F32 = jnp.float32
BF16 = jnp.bfloat16

EPS = 1e-6
ROPE_BASE = 10000.0
GRID_W = 64

S5_GROUP = 16
S5_STATE = 64
S5_T = 16

MLA_HEADS = 8
MLA_NOPE = 64
MLA_ROPE = 32
MLA_V = 64
HEAD_PAD = 128

RET_HEADS = 4
HG_HEADS = 4
HEAD_DIM = 128
CHUNK = 128
RET_DECAY_OFFSETS = (0.0, 0.5)
HG_BLOCK = 1

LANES = 128
ROW_TILE_TARGET = 1056
FF_TILE = 256
ATT_TQ = 256
ATT_TK = 768
ATT_SCORE_CHUNK = 128
ATT_PROB_CHUNK = 256


def _largest_tile(n, target, mult):
    best = None
    for t in range(mult, min(n, target) + 1, mult):
        if n % t == 0:
            best = t
    assert best is not None, (n, target, mult)
    return best


def _sigmoid(v):
    return 1.0 / (1.0 + jnp.exp(-v))


def _dot(a, b):
    return jnp.dot(a, b, preferred_element_type=F32)


def _dot_nt(a, b):
    return lax.dot_general(a, b, (((1,), (1,)), ((), ())), preferred_element_type=F32)


def _dot_tn(a, b):
    return lax.dot_general(a, b, (((0,), (0,)), ((), ())), preferred_element_type=F32)


def _mod_kernel(v_ref, w_ref, b_ref, o_ref):
    v = v_ref[...]
    sv = v * _sigmoid(v)
    o_ref[0] = jnp.dot(sv, w_ref[0], precision=lax.Precision.HIGHEST,
                       preferred_element_type=F32) + b_ref[0]


def _modulation(c, c_ctx, w_mod, b_mod):
    depth, d, n = w_mod.shape
    b = c.shape[0]
    rows = 8 * pl.cdiv(b + 1, 8)
    v = jnp.zeros((rows, d), F32).at[:b].set(c).at[b].set(c_ctx)
    tn = _largest_tile(n, 1536, LANES)
    out = pl.pallas_call(
        _mod_kernel,
        grid=(depth, n // tn),
        in_specs=[pl.BlockSpec((rows, d), lambda l, j: (0, 0)),
                  pl.BlockSpec((1, d, tn), lambda l, j: (l, 0, j)),
                  pl.BlockSpec((1, 1, tn), lambda l, j: (l, 0, j))],
        out_specs=pl.BlockSpec((1, rows, tn), lambda l, j: (l, 0, j)),
        out_shape=jax.ShapeDtypeStruct((depth, rows, n), F32),
    )(v, w_mod, b_mod.reshape(depth, 1, n))
    return out[:, :b + 1].reshape(depth, b + 1, 6, d)


def _modulated_norm(x, g, mx, mc, row0, lc, shift_row, scale_row):
    tm = x.shape[0]
    xn = x * lax.rsqrt(jnp.mean(x * x, axis=-1, keepdims=True) + EPS) * g
    row = row0 + lax.broadcasted_iota(jnp.int32, (tm, 1), 0)
    is_ctx = row < lc
    a = jnp.where(is_ctx, mc[scale_row:scale_row + 1, :], mx[scale_row:scale_row + 1, :])
    s = jnp.where(is_ctx, mc[shift_row:shift_row + 1, :], mx[shift_row:shift_row + 1, :])
    return xn * (1.0 + a) + s


def _in_proj_kernel(x_ref, mx_ref, mc_ref, g_ref, w_ref, cos_ref, sin_ref, o_ref, h_scr,
                    *, lc, tm, n_rope):
    i = pl.program_id(1)
    j = pl.program_id(2)

    @pl.when(j == 0)
    def _():
        h = _modulated_norm(x_ref[0], g_ref[...], mx_ref[0], mc_ref[0], i * tm, lc, 0, 1)
        h_scr[...] = h.astype(BF16)

    y = _dot(h_scr[...], w_ref[...])
    if n_rope == 0:
        o_ref[0] = y.astype(o_ref.dtype)
    else:
        @pl.when(j < n_rope)
        def _():
            cos = cos_ref[...]
            sin = sin_ref[...]
            for hh in range(y.shape[1] // HEAD_DIM):
                yh = y[:, hh * HEAD_DIM:(hh + 1) * HEAD_DIM]
                o_ref[0, :, hh * HEAD_DIM:(hh + 1) * HEAD_DIM] = (
                    yh * cos + pltpu.roll(yh, HEAD_DIM // 2, 1) * sin).astype(o_ref.dtype)

        @pl.when(j >= n_rope)
        def _():
            o_ref[0] = y.astype(o_ref.dtype)


def _in_proj(xa, mod, g, w, cos_t, sin_t, *, lc, tn, n_rope, out_dtype):
    b, s, d = xa.shape
    n = w.shape[1]
    tm = _largest_tile(s, ROW_TILE_TARGET, 16)
    return pl.pallas_call(
        functools.partial(_in_proj_kernel, lc=lc, tm=tm, n_rope=n_rope),
        grid=(b, s // tm, n // tn),
        in_specs=[pl.BlockSpec((1, tm, d), lambda bb, i, j: (bb, i, 0)),
                  pl.BlockSpec((1, 6, d), lambda bb, i, j: (bb, 0, 0)),
                  pl.BlockSpec((1, 6, d), lambda bb, i, j: (b, 0, 0)),
                  pl.BlockSpec((1, d), lambda bb, i, j: (0, 0)),
                  pl.BlockSpec((d, tn), lambda bb, i, j: (0, j)),
                  pl.BlockSpec((tm, HEAD_DIM), lambda bb, i, j: (i, 0)),
                  pl.BlockSpec((tm, HEAD_DIM), lambda bb, i, j: (i, 0))],
        out_specs=pl.BlockSpec((1, tm, tn), lambda bb, i, j: (bb, i, j)),
        out_shape=jax.ShapeDtypeStruct((b, s, n), out_dtype),
        scratch_shapes=[pltpu.VMEM((tm, d), BF16)],
        compiler_params=pltpu.CompilerParams(
            dimension_semantics=("parallel", "parallel", "arbitrary")),
    )(xa, mod, mod, g.reshape(1, d), w, cos_t, sin_t)


def _out_ffn_kernel(*refs, lc, tm, even, final):
    if even:
        (x_ref, ma_ref, mb_ref, u_ref, mx_ref, mc_ref, g_ref, wo_ref, w1_ref, w3_ref, w2_ref,
         fin_ref, sd_ref, wglu_ref, o_ref, x1_scr, h_scr, acc_scr) = refs
    else:
        (x_ref, ma_ref, mb_ref, mx_ref, mc_ref, g_ref, wo_ref, w1_ref, w3_ref, w2_ref,
         fin_ref, o_ref, x1_scr, h_scr, acc_scr) = refs
    i = pl.program_id(1)
    j = pl.program_id(2)
    half = ma_ref.shape[2]

    def mod_row(r):
        row = i * tm + lax.broadcasted_iota(jnp.int32, (tm, 1), 0)
        return jnp.where(row < lc, mc_ref[0, r:r + 1, :], mx_ref[0, r:r + 1, :])

    @pl.when(j == 0)
    def _():
        if even:
            z = jax.nn.gelu(sd_ref[...] * u_ref[0] + ma_ref[0])
            zb = z.astype(BF16)
            part_a = (z * _sigmoid(_dot(zb, wglu_ref[...]))).astype(BF16)
        else:
            part_a = ma_ref[0].astype(BF16)
        mixed = (_dot(part_a, wo_ref[0:half, :])
                 + _dot(mb_ref[0].astype(BF16), wo_ref[half:2 * half, :]))
        x1 = x_ref[0] + mod_row(2) * mixed
        x1_scr[...] = x1
        h = _modulated_norm(x1, g_ref[...], mx_ref[0], mc_ref[0], i * tm, lc, 3, 4)
        h_scr[...] = h.astype(BF16)
        acc_scr[...] = jnp.zeros_like(acc_scr)

    h = h_scr[...]
    a = _dot(h, w1_ref[...])
    gate = (a * _sigmoid(a)) * _dot(h, w3_ref[...])
    acc_scr[...] += _dot(gate.astype(BF16), w2_ref[...])

    @pl.when(j == pl.num_programs(2) - 1)
    def _():
        x2 = x1_scr[...] + mod_row(5) * acc_scr[...]
        if final:
            x2 = x2 * lax.rsqrt(jnp.mean(x2 * x2, axis=-1, keepdims=True) + EPS) * fin_ref[...]
        o_ref[0] = x2


def _out_ffn(xa, mix_a, mix_b, mod, g, w_out, w1, w3, w2, fin_g, *, lc, final,
             s5_u=None, s5_d=None, w_glu=None):
    b, s, d = xa.shape
    dff = w1.shape[1]
    half = mix_a.shape[2]
    even = s5_u is not None
    tm = _largest_tile(s, ROW_TILE_TARGET, 16)
    tf = _largest_tile(dff, FF_TILE, LANES)
    row_spec = lambda w: pl.BlockSpec((1, tm, w), lambda bb, i, j: (bb, i, 0))
    const2 = lambda shp: pl.BlockSpec(shp, lambda bb, i, j: (0, 0))
    in_specs = [row_spec(d), row_spec(half), row_spec(half)]
    args = [xa, mix_a, mix_b]
    if even:
        in_specs.append(row_spec(half))
        args.append(s5_u)
    in_specs += [pl.BlockSpec((1, 6, d), lambda bb, i, j: (bb, 0, 0)),
                 pl.BlockSpec((1, 6, d), lambda bb, i, j: (b, 0, 0)),
                 const2((1, d)), const2((2 * half, d)),
                 pl.BlockSpec((d, tf), lambda bb, i, j: (0, j)),
                 pl.BlockSpec((d, tf), lambda bb, i, j: (0, j)),
                 pl.BlockSpec((tf, d), lambda bb, i, j: (j, 0)),
                 const2((1, d))]
    args += [mod, mod, g.reshape(1, d), w_out, w1, w3, w2, fin_g.reshape(1, d)]
    if even:
        in_specs += [const2((1, half)), const2((half, half))]
        args += [s5_d.reshape(1, half), w_glu]
    return pl.pallas_call(
        functools.partial(_out_ffn_kernel, lc=lc, tm=tm, even=even, final=final),
        grid=(b, s // tm, dff // tf),
        in_specs=in_specs,
        out_specs=pl.BlockSpec((1, tm, d), lambda bb, i, j: (bb, i, 0)),
        out_shape=jax.ShapeDtypeStruct((b, s, d), F32),
        scratch_shapes=[pltpu.VMEM((tm, d), F32), pltpu.VMEM((tm, d), BF16),
                        pltpu.VMEM((tm, d), F32)],
        compiler_params=pltpu.CompilerParams(
            dimension_semantics=("parallel", "parallel", "arbitrary")),
    )(*args)


def _rope_angles(l, lc, dim):
    rows = l // GRID_W
    r, col = jnp.meshgrid(jnp.arange(rows, dtype=F32), jnp.arange(GRID_W, dtype=F32), indexing='ij')
    quarter = dim // 4
    inv = ROPE_BASE ** (-jnp.arange(quarter, dtype=F32) / quarter)
    ang = jnp.concatenate([r.reshape(-1, 1) * inv, col.reshape(-1, 1) * inv], axis=-1)
    cos = jnp.concatenate([jnp.ones((lc, dim // 2), F32), jnp.cos(ang)], axis=0)
    sin = jnp.concatenate([jnp.zeros((lc, dim // 2), F32), jnp.sin(ang)], axis=0)
    return cos, sin


def _split_pairs_perm(n):
    return np.concatenate([np.arange(0, n, 2), np.arange(1, n, 2)])


def _s5_matrices(a_re, a_im, log_dt, b_re, b_im, c_re, c_im):
    t_len = S5_T
    g, p = a_re.shape[1], a_re.shape[2]
    k = b_re.shape[3]
    tau = jnp.arange(t_len + 1, dtype=F32)
    toep = jnp.zeros((g, t_len, k, t_len, k), F32)
    f_parts, e_parts, a_chunk = [], [], []
    tt = jnp.arange(t_len)
    for r in range(2):
        dt = jnp.exp(log_dt[r])[:, None]
        mag = jnp.exp(a_re[r] * dt)
        ab_re = mag * jnp.cos(a_im[r] * dt)
        ab_im = mag * jnp.sin(a_im[r] * dt)
        nr, ni = ab_re - 1.0, ab_im
        den = a_re[r] * a_re[r] + a_im[r] * a_im[r]
        fr = (nr * a_re[r] + ni * a_im[r]) / den
        fi = (ni * a_re[r] - nr * a_im[r]) / den
        bb_re = fr[..., None] * b_re[r] - fi[..., None] * b_im[r]
        bb_im = fr[..., None] * b_im[r] + fi[..., None] * b_re[r]
        pw_mag = jnp.exp(tau[:, None, None] * (a_re[r] * dt)[None])
        pw_ang = tau[:, None, None] * (a_im[r] * dt)[None]
        pw_re = pw_mag * jnp.cos(pw_ang)
        pw_im = pw_mag * jnp.sin(pw_ang)
        ca_re = c_re[r][None] * pw_re[:, :, None, :] - c_im[r][None] * pw_im[:, :, None, :]
        ca_im = c_re[r][None] * pw_im[:, :, None, :] + c_im[r][None] * pw_re[:, :, None, :]
        m = jnp.sum(ca_re[..., None] * bb_re[None, :, None] - ca_im[..., None] * bb_im[None, :, None],
                    axis=3)
        lag = (tt[None, :] - tt[:, None]) if r == 0 else (tt[:, None] - tt[None, :])
        valid = lag >= 0
        mg = m[jnp.clip(lag, 0, t_len - 1)]
        mg = jnp.where(valid[:, :, None, None, None], mg, 0.0)
        toep = toep + mg.transpose(2, 0, 4, 1, 3)
        pidx = (t_len - 1 - tt) if r == 0 else tt
        f_re = pw_re[pidx][:, :, :, None] * bb_re[None] - pw_im[pidx][:, :, :, None] * bb_im[None]
        f_im = pw_re[pidx][:, :, :, None] * bb_im[None] + pw_im[pidx][:, :, :, None] * bb_re[None]
        f_parts += [f_re.transpose(1, 0, 3, 2).reshape(g, t_len * k, p),
                    f_im.transpose(1, 0, 3, 2).reshape(g, t_len * k, p)]
        qidx = (tt + 1) if r == 0 else (t_len - tt)
        e_parts += [ca_re[qidx].transpose(1, 3, 0, 2).reshape(g, p, t_len * k),
                    (-ca_im[qidx]).transpose(1, 3, 0, 2).reshape(g, p, t_len * k)]
        a_chunk += [pw_re[t_len].reshape(-1), pw_im[t_len].reshape(-1)]
    toep = toep.reshape(g, t_len * k, t_len * k).astype(BF16)
    pad_lo = lambda m_: jnp.concatenate([m_, jnp.zeros_like(m_)], axis=-1)
    pad_hi = lambda m_: jnp.concatenate([jnp.zeros_like(m_), m_], axis=-1)
    odd = (jnp.arange(g) % 2 == 1)[:, None, None]
    f_pad = jnp.stack([jnp.where(odd, pad_hi(m_), pad_lo(m_)) for m_ in f_parts], axis=1)
    pad_lo_r = lambda m_: jnp.concatenate([m_, jnp.zeros_like(m_)], axis=-2)
    pad_hi_r = lambda m_: jnp.concatenate([jnp.zeros_like(m_), m_], axis=-2)
    e_pad = jnp.stack([jnp.where(odd, pad_hi_r(m_), pad_lo_r(m_)) for m_ in e_parts], axis=1)
    return toep, f_pad.astype(BF16), e_pad.astype(BF16), jnp.stack(a_chunk, axis=0)


def _s5_inject_kernel(u_ref, f_ref, o0, o1, o2, o3):
    u0 = u_ref[0]
    u1 = u_ref[1]
    for a, o in enumerate((o0, o1, o2, o3)):
        o[...] = _dot(u0, f_ref[0, a]) + _dot(u1, f_ref[1, a])


def _s5_scan_kernel(s0, s1, s2, s3, a_ref, h0, h1, h2, h3, *, nb, nc, ncc):
    a = a_ref[...]
    arf, aif, arb, aib = a[0:1], a[1:2], a[2:3], a[3:4]
    ct = s0.shape[1]

    def body(i, carry):
        cb = jnp.where(i < ncc, ncc - 1 - i, nc - 1 - (i - ncc))
        new = []
        for bb in range(nb):
            hr, hi, gr, gi = carry[4 * bb:4 * bb + 4]
            rf = bb * nc + i
            rb = bb * nc + cb
            h0[pl.ds(rf, 1), :] = hr
            h1[pl.ds(rf, 1), :] = hi
            h2[pl.ds(rb, 1), :] = gr
            h3[pl.ds(rb, 1), :] = gi
            new += [arf * hr - aif * hi + s0[pl.ds(rf, 1), :],
                    arf * hi + aif * hr + s1[pl.ds(rf, 1), :],
                    arb * gr - aib * gi + s2[pl.ds(rb, 1), :],
                    arb * gi + aib * gr + s3[pl.ds(rb, 1), :]]
        return tuple(new)

    zero = jnp.zeros((1, ct), F32)
    lax.fori_loop(0, nc, body, (zero,) * (4 * nb))


def _s5_readout_kernel(u_ref, t_ref, h0, h1, h2, h3, e_ref, o_ref):
    y = _dot(u_ref[0], t_ref[0])
    for a, h in enumerate((h0, h1, h2, h3)):
        y = y + _dot(h[...].astype(BF16), e_ref[0, a])
    o_ref[0] = y


def _s5_mixer(proj, mats, *, lc):
    toep, f_pad, e_pad, a_chunk = mats
    b, s, _ = proj.shape
    g = toep.shape[0]
    k = S5_GROUP
    width = g * k
    nc = s // S5_T
    ncc = lc // S5_T
    rows = b * nc
    tk = S5_T * k
    u = proj[:, :, :width].reshape(b, nc, S5_T, g, k).transpose(3, 0, 1, 2, 4)
    u = u.reshape(g, rows, tk).astype(BF16)
    sw = g * S5_STATE
    st_shape = jax.ShapeDtypeStruct((rows, sw), F32)
    inj = pl.pallas_call(
        _s5_inject_kernel,
        grid=(g // 2,),
        in_specs=[pl.BlockSpec((2, rows, tk), lambda j: (j, 0, 0)),
                  pl.BlockSpec((2, 4, tk, LANES), lambda j: (j, 0, 0, 0))],
        out_specs=[pl.BlockSpec((rows, LANES), lambda j: (0, j))] * 4,
        out_shape=[st_shape] * 4,
    )(u, f_pad)
    ct = _largest_tile(sw, 512, LANES)
    states = pl.pallas_call(
        functools.partial(_s5_scan_kernel, nb=b, nc=nc, ncc=ncc),
        grid=(sw // ct,),
        in_specs=[pl.BlockSpec((rows, ct), lambda j: (0, j))] * 4
        + [pl.BlockSpec((4, ct), lambda j: (0, j))],
        out_specs=[pl.BlockSpec((rows, ct), lambda j: (0, j))] * 4,
        out_shape=[st_shape] * 4,
    )(*inj, a_chunk)
    y = pl.pallas_call(
        _s5_readout_kernel,
        grid=(g,),
        in_specs=[pl.BlockSpec((1, rows, tk), lambda j: (j, 0, 0)),
                  pl.BlockSpec((1, tk, tk), lambda j: (j, 0, 0))]
        + [pl.BlockSpec((rows, LANES), lambda j: (0, j // 2))] * 4
        + [pl.BlockSpec((1, 4, LANES, tk), lambda j: (j, 0, 0, 0))],
        out_specs=pl.BlockSpec((1, rows, tk), lambda j: (j, 0, 0)),
        out_shape=jax.ShapeDtypeStruct((g, rows, tk), F32),
    )(u, toep, *states, e_pad)
    y = y.reshape(g, b, nc, S5_T, k).transpose(1, 2, 3, 0, 4)
    return y.reshape(b, s, width)


def _mla_weights(w_uq, w_ukv):
    rq = w_uq.shape[0]
    rkv = w_ukv.shape[0]
    hq = w_uq.reshape(rq, MLA_HEADS, MLA_NOPE + MLA_ROPE)
    nope = hq[:, :, :MLA_NOPE]
    r_even = hq[:, :, MLA_NOPE::2]
    r_odd = hq[:, :, MLA_NOPE + 1::2]
    zq = jnp.zeros((rq, MLA_HEADS, HEAD_PAD - MLA_NOPE - MLA_ROPE), F32)
    wq = jnp.concatenate([nope, r_even, r_odd, zq], axis=-1)
    wq_sw = jnp.concatenate([jnp.zeros_like(nope), r_odd, r_even, zq], axis=-1)
    hkv = w_ukv.reshape(rkv, MLA_HEADS, MLA_NOPE + MLA_V)
    zk = jnp.zeros((rkv, MLA_HEADS, HEAD_PAD - MLA_NOPE), F32)
    wk = jnp.concatenate([hkv[:, :, :MLA_NOPE], zk], axis=-1)
    wv = jnp.concatenate([hkv[:, :, MLA_NOPE:], jnp.zeros((rkv, MLA_HEADS, HEAD_PAD - MLA_V), F32)],
                         axis=-1)
    out_in = lambda w: w.transpose(1, 2, 0).astype(BF16)
    return out_in(wq), out_in(wq_sw), wk.transpose(1, 0, 2).astype(BF16), out_in(wv)


def _mla_tables(l, lc):
    cos, sin = _rope_angles(l, lc, MLA_ROPE)
    s = cos.shape[0]
    z_lo = jnp.zeros((s, MLA_NOPE), F32)
    z_hi = jnp.zeros((s, HEAD_PAD - MLA_NOPE - MLA_ROPE), F32)
    ck = jnp.concatenate([z_lo, cos, cos, z_hi], axis=-1)
    sk = jnp.concatenate([z_lo, -sin, sin, z_hi], axis=-1)
    scale = (MLA_NOPE + MLA_ROPE) ** -0.5 * math.log2(math.e)
    cq = jnp.concatenate([jnp.ones_like(z_lo), cos, cos, z_hi], axis=-1) * scale
    return cq.T, sk.T * scale, ck, sk


def _mla_proj_kernel(cq_ref, ckv_ref, kr_ref, krs_ref, qn_ref, kvn_ref, wqt_ref, wqst_ref, wk_ref,
                     wvt_ref, tcq_ref, tsq_ref, tck_ref, tsk_ref, qt_ref, k_ref, vt_ref):
    def norm(v, g):
        return (v * lax.rsqrt(jnp.mean(v * v, axis=-1, keepdims=True) + EPS) * g).astype(BF16)

    hq = norm(cq_ref[0], qn_ref[...])
    hkv = norm(ckv_ref[0], kvn_ref[...])
    k_rope = kr_ref[0] * tck_ref[...] + krs_ref[0] * tsk_ref[...]
    row = lax.broadcasted_iota(jnp.int32, (HEAD_PAD, 1), 0)
    ones_row = (row == MLA_V).astype(F32)
    tcq = tcq_ref[...]
    tsq = tsq_ref[...]
    for hh in range(MLA_HEADS):
        qt = _dot_nt(wqt_ref[hh], hq) * tcq + _dot_nt(wqst_ref[hh], hq) * tsq
        qt_ref[0, hh] = qt.astype(BF16)
        k_ref[0, hh] = (_dot(hkv, wk_ref[hh]) + k_rope).astype(BF16)
        vt_ref[0, hh, 0] = (_dot_nt(wvt_ref[hh], hkv) + ones_row).astype(BF16)


def _attn_kernel(qt_ref, k_ref, vt_ref, o_ref, st_scr, acc_scr, *, lc, tq, tk, s_total):
    heads = qt_ref.shape[1]
    kc = ATT_SCORE_CHUNK
    pc = ATT_PROB_CHUNK
    sub = 8

    def score_chunk(slot, hh, first, c, part):
        row0 = first + c * kc
        if not isinstance(row0, int):
            row0 = pl.multiple_of(row0, kc)
        st = _dot(k_ref[0, hh, pl.ds(row0, kc), :], qt_ref[0, hh])
        st_scr[slot, hh, c * kc:(c + 1) * kc, :] = st
        cm = jnp.max(st.reshape(kc // sub, sub, tq), axis=0)
        return cm if part is None else jnp.maximum(part, cm)

    def prob_chunk(slot, hh, tile, j, m, alpha):
        p = jnp.exp2(st_scr[slot, hh, j * pc:(j + 1) * pc, :] - m).astype(BF16)
        pv = _dot(vt_ref[0, hh, tile, :, j * pc:(j + 1) * pc], p)
        if j == 0:
            acc_scr[hh] = acc_scr[hh] * alpha + pv
        else:
            acc_scr[hh] += pv

    def fold_max(m_old, part):
        m_new = jnp.maximum(m_old, jnp.max(part, axis=0, keepdims=True))
        return m_new, jnp.exp2(m_old - m_new)

    def phase(nkeys, a_args, bc_args):
        parts = [None] * heads
        assert pc // kc == heads and nkeys % pc == 0
        for j in range(nkeys // pc):
            for r in range(heads):
                if a_args is not None:
                    for hh in range(heads):
                        parts[hh] = score_chunk(a_args[0], hh, a_args[1], j * heads + r, parts[hh])
                if bc_args is not None:
                    slot, tile, ms, alphas = bc_args
                    prob_chunk(slot, r, tile, j, ms[r], alphas[r])
        return parts

    def finish():
        out_t = jnp.concatenate(
            [acc_scr[hh, 0:MLA_V, :] / acc_scr[hh, MLA_V:MLA_V + 1, :] for hh in range(heads)],
            axis=0)
        o_ref[0] = out_t.T

    def attend(nkeys, n_tiles):
        acc_scr[...] = jnp.zeros(acc_scr.shape, F32)
        neg = jnp.full((1, tq), -jnp.inf, F32)
        parts = phase(nkeys, (0, 0), None)
        state = []
        for hh in range(heads):
            state += list(fold_max(neg, parts[hh]))

        def step(t, slot, carry):
            ms, alphas = carry[0::2], carry[1::2]
            nxt = phase(nkeys, (1 - slot, (t + 1) * nkeys), (slot, t, ms, alphas))
            new = []
            for hh in range(heads):
                new += list(fold_max(ms[hh], nxt[hh]))
            return tuple(new)

        pairs = (n_tiles - 1) // 2
        state = tuple(state)
        if pairs > 0:
            state = lax.fori_loop(
                0, pairs, lambda i, carry: step(2 * i + 1, 1, step(2 * i, 0, carry)), state)
        for t in range(2 * pairs, n_tiles - 1):
            state = step(t, t % 2, state)
        phase(nkeys, None, ((n_tiles - 1) % 2, n_tiles - 1, state[0::2], state[1::2]))
        finish()

    is_ctx = pl.program_id(2) < lc // tq

    @pl.when(is_ctx)
    def _():
        attend(lc, 1)

    @pl.when(jnp.logical_not(is_ctx))
    def _():
        attend(tk, s_total // tk)


def _mla_mixer(proj, q_norm, kv_norm, weights, tables, *, lc):
    b, s, _ = proj.shape
    wqt, wqst, wk, wvt = weights
    tq = ATT_TQ
    tk = _largest_tile(s, ATT_TK, ATT_PROB_CHUNK)
    assert lc % tq == 0 and s % tq == 0 and lc <= tk and lc % ATT_PROB_CHUNK == 0
    rq = wqt.shape[2]
    rkv = wk.shape[1]
    base = 512
    head_shape = jax.ShapeDtypeStruct((b, MLA_HEADS, s, HEAD_PAD), BF16)
    full = lambda shp: pl.BlockSpec(shp, lambda bb, i: tuple(0 for _ in shp))
    tab = pl.BlockSpec((tk, HEAD_PAD), lambda bb, i: (i, 0))
    tab_t = pl.BlockSpec((HEAD_PAD, tk), lambda bb, i: (0, i))
    head_rows = pl.BlockSpec((1, MLA_HEADS, tk, HEAD_PAD), lambda bb, i: (bb, 0, i, 0))
    qt, k, vt = pl.pallas_call(
        _mla_proj_kernel,
        grid=(b, s // tk),
        in_specs=[pl.BlockSpec((1, tk, rq), lambda bb, i: (bb, i, base // rq)),
                  pl.BlockSpec((1, tk, rkv), lambda bb, i: (bb, i, (base + rq) // rkv)),
                  pl.BlockSpec((1, tk, LANES), lambda bb, i: (bb, i, (base + rq + rkv) // LANES)),
                  pl.BlockSpec((1, tk, LANES), lambda bb, i: (bb, i, (base + rq + rkv) // LANES + 1)),
                  full((1, rq)), full((1, rkv)),
                  full(wqt.shape), full(wqst.shape), full(wk.shape), full(wvt.shape),
                  tab_t, tab_t, tab, tab],
        out_specs=[pl.BlockSpec((1, MLA_HEADS, HEAD_PAD, tk), lambda bb, i: (bb, 0, 0, i)), head_rows,
                   pl.BlockSpec((1, MLA_HEADS, 1, HEAD_PAD, tk), lambda bb, i: (bb, 0, i, 0, 0))],
        out_shape=[jax.ShapeDtypeStruct((b, MLA_HEADS, HEAD_PAD, s), BF16), head_shape,
                   jax.ShapeDtypeStruct((b, MLA_HEADS, s // tk, HEAD_PAD, tk), BF16)],
    )(proj, proj, proj, proj, q_norm.reshape(1, rq), kv_norm.reshape(1, rkv),
      wqt, wqst, wk, wvt, *tables)
    return pl.pallas_call(
        functools.partial(_attn_kernel, lc=lc, tq=tq, tk=tk, s_total=s),
        grid=(b, MLA_HEADS // 2, s // tq),
        in_specs=[pl.BlockSpec((1, 2, HEAD_PAD, tq), lambda bb, hp, i: (bb, hp, 0, i)),
                  pl.BlockSpec((1, 2, s, HEAD_PAD), lambda bb, hp, i: (bb, hp, 0, 0)),
                  pl.BlockSpec((1, 2, s // tk, HEAD_PAD, tk), lambda bb, hp, i: (bb, hp, 0, 0, 0))],
        out_specs=pl.BlockSpec((1, tq, 2 * MLA_V), lambda bb, hp, i: (bb, i, hp)),
        out_shape=jax.ShapeDtypeStruct((b, s, MLA_HEADS * MLA_V), F32),
        scratch_shapes=[pltpu.VMEM((2, 2, tk, tq), F32), pltpu.VMEM((2, HEAD_PAD, tq), F32)],
        compiler_params=pltpu.CompilerParams(
            dimension_semantics=("parallel", "parallel", "arbitrary")),
    )(qt, k, vt)


def _state_scan_kernel(kv_ref, dec_ref, st_ref, *, nch, ncc):
    hd = HEAD_DIM

    def body(i, carry):
        sf, sb = carry
        cb = jnp.where(i < ncc, ncc - 1 - i, nch - 1 - (i - ncc))
        st_ref[0, 0, i, :, 0:hd] = sf.astype(BF16)
        st_ref[0, 0, cb, :, hd:2 * hd] = sb.astype(BF16)
        sf = dec_ref[0, 0, i, :, 0:hd] * sf + kv_ref[0, 0, i, :, 0:hd]
        sb = dec_ref[0, 0, cb, :, hd:2 * hd] * sb + kv_ref[0, 0, cb, :, hd:2 * hd]
        return sf, sb

    zero = jnp.zeros((hd, hd), F32)
    lax.fori_loop(0, nch, body, (zero, zero))


def _state_scan(kv, dec, *, ncc):
    b, h, nch = kv.shape[:3]
    hd = HEAD_DIM
    return pl.pallas_call(
        functools.partial(_state_scan_kernel, nch=nch, ncc=ncc),
        grid=(b, h),
        in_specs=[pl.BlockSpec((1, 1, nch, hd, 2 * hd), lambda bb, hh: (bb, hh, 0, 0, 0)),
                  pl.BlockSpec((1, 1, nch, 1, 2 * hd), lambda bb, hh: (bb, hh, 0, 0, 0))],
        out_specs=pl.BlockSpec((1, 1, nch, hd, 2 * hd), lambda bb, hh: (bb, hh, 0, 0, 0)),
        out_shape=jax.ShapeDtypeStruct((b, h, nch, hd, 2 * hd), BF16),
    )(kv, dec)


def _readout_kernel(oi_ref, qd_ref, st_ref, g_ref, gn_ref, o_ref, *, cpt, center):
    def body(c, _):
        r0 = pl.multiple_of(c * CHUNK, CHUNK)
        o = oi_ref[0, 0, pl.ds(r0, CHUNK), :] + _dot_nt(qd_ref[0, 0, pl.ds(r0, CHUNK), :],
                                                        st_ref[0, 0, c])
        if center:
            o = o - jnp.mean(o, axis=-1, keepdims=True)
        o = o * lax.rsqrt(jnp.mean(o * o, axis=-1, keepdims=True) + EPS) * gn_ref[...]
        gate = g_ref[0, pl.ds(r0, CHUNK), :].astype(F32)
        o_ref[0, pl.ds(r0, CHUNK), :] = o * (gate * _sigmoid(gate))
        return 0

    lax.fori_loop(0, cpt, body, 0, unroll=True)


def _readout(o_intra, q_dec, states, proj, gate_col, gn, *, center):
    b, h, s, hd = o_intra.shape
    nch = s // CHUNK
    cpt = _largest_tile(nch, 11, 1)
    tl = cpt * CHUNK
    return pl.pallas_call(
        functools.partial(_readout_kernel, cpt=cpt, center=center),
        grid=(b, h, nch // cpt),
        in_specs=[pl.BlockSpec((1, 1, tl, hd), lambda bb, hh, t: (bb, hh, t, 0)),
                  pl.BlockSpec((1, 1, tl, 2 * hd), lambda bb, hh, t: (bb, hh, t, 0)),
                  pl.BlockSpec((1, 1, cpt, hd, 2 * hd), lambda bb, hh, t: (bb, hh, t, 0, 0)),
                  pl.BlockSpec((1, tl, hd), lambda bb, hh, t: (bb, t, gate_col + hh)),
                  pl.BlockSpec((1, hd), lambda bb, hh, t: (0, hh))],
        out_specs=pl.BlockSpec((1, tl, hd), lambda bb, hh, t: (bb, t, hh)),
        out_shape=jax.ShapeDtypeStruct((b, s, h * hd), F32),
    )(o_intra, q_dec, states, proj, gn.reshape(1, h * hd))


def _local_out_shapes(b, h, s):
    nch = s // CHUNK
    hd = HEAD_DIM
    return [jax.ShapeDtypeStruct((b, h, s, hd), F32),
            jax.ShapeDtypeStruct((b, h, s, 2 * hd), BF16),
            jax.ShapeDtypeStruct((b, h, nch, hd, 2 * hd), F32)]


def _local_out_specs(cpt):
    tl = cpt * CHUNK
    hd = HEAD_DIM
    return [pl.BlockSpec((1, 1, tl, hd), lambda bb, hh, t: (bb, hh, t, 0)),
            pl.BlockSpec((1, 1, tl, 2 * hd), lambda bb, hh, t: (bb, hh, t, 0)),
            pl.BlockSpec((1, 1, cpt, hd, 2 * hd), lambda bb, hh, t: (bb, hh, t, 0, 0))]


def _ret_constants():
    c = CHUNK
    pos = np.arange(c, dtype=np.float64)
    rel = pos[:, None] - pos[None, :]
    scale = HEAD_DIM ** -0.5
    dmask = np.zeros((RET_HEADS, c, c))
    qw = np.zeros((RET_HEADS, c, 2 * HEAD_DIM))
    kw = np.zeros((RET_HEADS, c, 2 * HEAD_DIM))
    dec = np.zeros((RET_HEADS, 1, 2 * HEAD_DIM))
    for r, offset in enumerate(RET_DECAY_OFFSETS):
        lg = np.log1p(-np.exp2(-(5.0 + offset) - np.arange(RET_HEADS, dtype=np.float64)))[:, None, None]
        dist = rel if r == 0 else -rel
        dmask += np.where(dist >= 0, np.exp(lg * np.maximum(dist, 0.0)), 0.0) * scale
        steps_in = (pos + 1) if r == 0 else (c - pos)
        steps_out = (c - 1 - pos) if r == 0 else pos
        sl = slice(r * HEAD_DIM, (r + 1) * HEAD_DIM)
        qw[:, :, sl] = np.exp(lg[:, :, 0] * steps_in[None, :])[:, :, None]
        kw[:, :, sl] = np.exp(lg[:, :, 0] * steps_out[None, :])[:, :, None] * scale
        dec[:, :, sl] = np.exp(lg * c)
    return (jnp.asarray(dmask, F32), jnp.asarray(qw, F32), jnp.asarray(kw, F32), jnp.asarray(dec, F32))


def _ret_local_kernel(q_ref, k_ref, v_ref, dm_ref, qw_ref, kw_ref, oi_ref, qd_ref, kv_ref, *, cpt):
    dm = dm_ref[0]
    qw = qw_ref[0]
    kw = kw_ref[0]

    def body(c, _):
        r0 = pl.multiple_of(c * CHUNK, CHUNK)
        q = q_ref[0, pl.ds(r0, CHUNK), :].astype(F32)
        k = k_ref[0, pl.ds(r0, CHUNK), :].astype(F32)
        v = v_ref[0, pl.ds(r0, CHUNK), :].astype(BF16)
        inner = _dot_nt(q.astype(BF16), k.astype(BF16)) * dm
        oi_ref[0, 0, pl.ds(r0, CHUNK), :] = _dot(inner.astype(BF16), v)
        q2 = jnp.concatenate([q, q], axis=-1) * qw
        qd_ref[0, 0, pl.ds(r0, CHUNK), :] = q2.astype(BF16)
        k2 = (jnp.concatenate([k, k], axis=-1) * kw).astype(BF16)
        kv_ref[0, 0, c] = _dot_tn(v, k2)
        return 0

    lax.fori_loop(0, cpt, body, 0, unroll=True)


def _retention_mixer(proj, gn, consts, *, lc):
    b, s, _ = proj.shape
    h = RET_HEADS
    hd = HEAD_DIM
    dmask, qw, kw, dec = consts
    nch = s // CHUNK
    cpt = _largest_tile(nch, 11, 1)
    tl = cpt * CHUNK
    col = lambda base: pl.BlockSpec((1, tl, hd), lambda bb, hh, t: (bb, t, base + hh))
    per_head = lambda w: pl.BlockSpec((1, CHUNK, w), lambda bb, hh, t: (hh, 0, 0))
    o_intra, q_dec, kv = pl.pallas_call(
        functools.partial(_ret_local_kernel, cpt=cpt),
        grid=(b, h, nch // cpt),
        in_specs=[col(0), col(h), col(2 * h), per_head(CHUNK), per_head(2 * hd), per_head(2 * hd)],
        out_specs=_local_out_specs(cpt),
        out_shape=_local_out_shapes(b, h, s),
    )(proj, proj, proj, dmask, qw, kw)
    dec_all = jnp.broadcast_to(dec[None, :, None], (b, h, nch, 1, 2 * hd))
    states = _state_scan(kv, dec_all, ncc=lc // CHUNK)
    return _readout(o_intra, q_dec, states, proj, 3 * h, gn, center=True)


HG_LEVELS = (1, 2, 4, 8, 16, 32, 64)


def _hg_constants():
    c = CHUNK
    t = np.arange(c)[:, None]
    s = np.arange(c)[None, :]
    sums, masks = [], []
    for blk in HG_LEVELS:
        same = (t // blk) == (s // blk)
        right = (t // blk) % 2 == 1
        sums.append((right & same & (s <= t)) | (~right & same & (s > t)))
        masks.append(((t // (2 * blk)) == (s // (2 * blk))) & right & ((s // blk) % 2 == 0))
    sums.append(np.broadcast_to(s <= t, (c, c)))
    sums.append(np.broadcast_to(s > t, (c, c)))
    if HG_BLOCK > 1:
        sums.append(((t // HG_BLOCK) == (s // HG_BLOCK)) & (s <= t))
    g_f = np.concatenate([m.astype(np.float32) for m in sums], axis=0)
    m_f = np.stack([m.astype(np.float32) for m in masks], axis=0)
    flip = lambda a: a[..., ::-1, ::-1]
    g_b = np.concatenate([flip(m.astype(np.float32)) for m in sums], axis=0)
    m_b = np.stack([flip(m.astype(np.float32)) for m in masks], axis=0)
    return (jnp.asarray(np.stack([g_f, g_b]), BF16), jnp.asarray(np.stack([m_f, m_b]), F32))


def _hg_local_kernel(q_ref, ff_ref, fb_ref, v_ref, lb_ref, gs_ref, mk_ref, oi_ref, qd_ref, kv_ref,
                     dec_ref, *, cpt):
    c_len = CHUNK
    nlev = len(HG_LEVELS)
    lb = lb_ref[...]
    row = lax.broadcasted_iota(jnp.int32, (c_len, c_len), 0)
    col = lax.broadcasted_iota(jnp.int32, (c_len, c_len), 1)
    row_in_blk = row % HG_BLOCK

    def body(c, _):
        r0 = pl.multiple_of(c * c_len, c_len)
        q = q_ref[0, pl.ds(r0, c_len), :].astype(F32)
        v = v_ref[0, pl.ds(r0, c_len), :].astype(BF16)
        att = jnp.zeros((c_len, c_len), F32)
        q_parts, kv_parts, dec_parts = [], [], []
        for d, f_ref in enumerate((ff_ref, fb_ref)):
            f = lb + (1.0 - lb) * _sigmoid(f_ref[0, pl.ds(r0, c_len), :].astype(F32))
            kk = 1.0 - f
            z = _dot(gs_ref[d], jnp.log(f).astype(BF16))
            for lv in range(nlev):
                e = jnp.exp(z[lv * c_len:(lv + 1) * c_len])
                att = att + mk_ref[d, lv] * _dot_nt((q * e).astype(BF16), (kk * e).astype(BF16))
            zq = z[nlev * c_len:(nlev + 1) * c_len]
            zk = z[(nlev + 1) * c_len:(nlev + 2) * c_len]
            cl = z[(nlev + 2) * c_len:(nlev + 3) * c_len] if HG_BLOCK > 1 else None
            for delta in range(HG_BLOCK):
                if delta == 0:
                    prod = q * kk
                else:
                    sh = delta if d == 0 else c_len - delta
                    prod = (q * pltpu.roll(kk, sh, 0)
                            * jnp.exp(jnp.minimum(cl - pltpu.roll(cl, sh, 0), 0.0)))
                a = jnp.sum(prod, axis=-1, keepdims=True)
                if d == 0:
                    hit = (col == row - delta) & (row_in_blk >= delta)
                else:
                    hit = (col == row + delta) & (row_in_blk + delta < HG_BLOCK)
                att = att + jnp.where(hit, a, 0.0)
            q_parts.append(q * jnp.exp(zq))
            kv_parts.append(kk * jnp.exp(zk))
            last = c_len - 1 if d == 0 else 0
            dec_parts.append(jnp.exp(zq[last:last + 1]))
        oi_ref[0, 0, pl.ds(r0, c_len), :] = _dot(att.astype(BF16), v)
        qd_ref[0, 0, pl.ds(r0, c_len), :] = jnp.concatenate(q_parts, axis=-1).astype(BF16)
        kv_ref[0, 0, c] = _dot_tn(v, jnp.concatenate(kv_parts, axis=-1).astype(BF16))
        dec_ref[0, 0, c] = jnp.concatenate(dec_parts, axis=-1)
        return 0

    lax.fori_loop(0, cpt, body, 0, unroll=2)


def _hgrn2_mixer(proj, lb, gn, consts, *, lc, col0):
    b, s, _ = proj.shape
    h = HG_HEADS
    hd = HEAD_DIM
    gsum, masks = consts
    nch = s // CHUNK
    cpt = _largest_tile(nch, 11, 1)
    tl = cpt * CHUNK
    col = lambda base: pl.BlockSpec((1, tl, hd), lambda bb, hh, t: (bb, t, col0 + base + hh))
    full = lambda a: pl.BlockSpec(a.shape, lambda bb, hh, t: tuple(0 for _ in a.shape))
    o_intra, q_dec, kv, dec = pl.pallas_call(
        functools.partial(_hg_local_kernel, cpt=cpt),
        grid=(b, h, nch // cpt),
        in_specs=[col(0), col(h), col(2 * h), col(3 * h),
                  pl.BlockSpec((1, hd), lambda bb, hh, t: (0, hh)), full(gsum), full(masks)],
        out_specs=_local_out_specs(cpt)
        + [pl.BlockSpec((1, 1, cpt, 1, 2 * hd), lambda bb, hh, t: (bb, hh, t, 0, 0))],
        out_shape=_local_out_shapes(b, h, s)
        + [jax.ShapeDtypeStruct((b, h, nch, 1, 2 * hd), F32)],
    )(proj, proj, proj, proj, lb.reshape(1, h * hd), gsum, masks)
    states = _state_scan(kv, dec, ncc=lc // CHUNK)
    return _readout(o_intra, q_dec, states, proj, col0 + 4 * h, gn, center=False)


def kernel(x, c, ctx, c_ctx, w_mod, b_mod, norm1_g, norm2_g, ffn_w1, ffn_w3, ffn_w2, w_in_even, w_out_even, s5_a_re, s5_a_im, s5_log_dt, s5_b_re, s5_b_im, s5_c_re, s5_c_im, s5_d, s5_w_glu, mla_q_norm, mla_w_uq, mla_kv_norm, mla_w_ukv, w_in_odd, w_out_odd, ret_gn, hg_lb_logits, hg_gn, final_norm):
    b, l, d = x.shape
    lc = ctx.shape[1]
    depth = w_mod.shape[0]
    n_odd = w_in_odd.shape[0]
    assert lc % CHUNK == 0 and l % CHUNK == 0 and l % GRID_W == 0

    xa = jnp.concatenate([ctx, x], axis=1)
    mods = _modulation(c, c_ctx, w_mod, b_mod)

    cos_r, sin_r = _rope_angles(l, lc, HEAD_DIM)
    ret_cos = jnp.concatenate([cos_r, cos_r], axis=-1)
    ret_sin = jnp.concatenate([-sin_r, sin_r], axis=-1)
    mla_tabs = _mla_tables(l, lc)
    ret_consts = _ret_constants()
    hg_consts = _hg_constants()
    hg_lb = jnp.cumsum(jax.nn.softmax(hg_lb_logits.astype(F32), axis=0), axis=0)[:n_odd]

    s5_w = s5_d.shape[1]
    kr0 = w_in_even.shape[2] - MLA_ROPE
    kr_e = w_in_even[:, :, kr0::2]
    kr_o = w_in_even[:, :, kr0 + 1::2]
    z_lo = jnp.zeros(w_in_even.shape[:2] + (MLA_NOPE,), F32)
    z_hi = jnp.zeros(w_in_even.shape[:2] + (HEAD_PAD - MLA_NOPE - MLA_ROPE,), F32)
    w_in_e = jnp.concatenate([w_in_even[:, :, :kr0], z_lo, kr_e, kr_o, z_hi, z_lo, kr_o, kr_e, z_hi],
                             axis=-1).astype(BF16)
    qk_w = 2 * RET_HEADS * HEAD_DIM
    perm = (np.arange(qk_w) // HEAD_DIM) * HEAD_DIM + np.tile(_split_pairs_perm(HEAD_DIM),
                                                               qk_w // HEAD_DIM)
    w_in_o = jnp.concatenate([w_in_odd[:, :, perm], w_in_odd[:, :, qk_w:]], axis=-1).astype(BF16)

    s5_mats = jax.vmap(_s5_matrices)(s5_a_re, s5_a_im, s5_log_dt, s5_b_re, s5_b_im, s5_c_re, s5_c_im)
    mla_w = jax.vmap(_mla_weights)(mla_w_uq, mla_w_ukv)

    w1 = ffn_w1.astype(BF16)
    w3 = ffn_w3.astype(BF16)
    w2 = ffn_w2.astype(BF16)
    w_out_e = w_out_even.astype(BF16)
    w_out_o = w_out_odd.astype(BF16)
    w_glu = s5_w_glu.astype(BF16)

    for li in range(depth):
        j = li // 2
        final = li == depth - 1
        if li % 2 == 0:
            proj = _in_proj(xa, mods[li], norm1_g[li], w_in_e[j], ret_cos, ret_sin,
                            lc=lc, tn=w_in_e.shape[2], n_rope=0, out_dtype=F32)
            y_s5 = _s5_mixer(proj, tuple(m[j] for m in s5_mats), lc=lc)
            attn = _mla_mixer(proj, mla_q_norm[j], mla_kv_norm[j],
                              tuple(w[j] for w in mla_w), mla_tabs, lc=lc)
            xa = _out_ffn(xa, y_s5, attn, mods[li], norm2_g[li], w_out_e[j], w1[li], w3[li], w2[li],
                          final_norm, lc=lc, final=final, s5_u=proj, s5_d=s5_d[j], w_glu=w_glu[j])
        else:
            proj = _in_proj(xa, mods[li], norm1_g[li], w_in_o[j], ret_cos, ret_sin,
                            lc=lc, tn=RET_HEADS * HEAD_DIM, n_rope=2, out_dtype=BF16)
            ret = _retention_mixer(proj, ret_gn[j], ret_consts, lc=lc)
            hg = _hgrn2_mixer(proj, hg_lb[j], hg_gn[j], hg_consts, lc=lc, col0=4 * RET_HEADS)
            xa = _out_ffn(xa, ret, hg, mods[li], norm2_g[li], w_out_o[j], w1[li], w3[li], w2[li],
                          final_norm, lc=lc, final=final)
    return xa[:, lc:, :]
```

```python
import functools
import math

import jax
import jax.numpy as jnp
import numpy as np
from jax import lax
from jax.experimental import pallas as pl
from jax.experimental.pallas import tpu as pltpu

F32 = jnp.float32
BF16 = jnp.bfloat16

EPS = 1e-6
ROPE_BASE = 10000.0
GRID_W = 64

S5_GROUP = 16
S5_STATE = 64
S5_T = 16

MLA_HEADS = 8
MLA_NOPE = 64
MLA_ROPE = 32
MLA_V = 64
HEAD_PAD = 128

RET_HEADS = 4
HG_HEADS = 4
HEAD_DIM = 128
CHUNK = 128
RET_DECAY_OFFSETS = (0.0, 0.5)
HG_BLOCK = 1

LANES = 128
ROW_TILE_TARGET = 1056
FF_TILE = 256
ATT_TQ = 256
ATT_TK = 768
ATT_SCORE_CHUNK = 128
ATT_PROB_CHUNK = 256


def _largest_tile(n, target, mult):
    best = None
    for t in range(mult, min(n, target) + 1, mult):
        if n % t == 0:
            best = t
    assert best is not None, (n, target, mult)
    return best


def _sigmoid(v):
    return 1.0 / (1.0 + jnp.exp(-v))


def _dot(a, b):
    return jnp.dot(a, b, preferred_element_type=F32)


def _dot_nt(a, b):
    return lax.dot_general(a, b, (((1,), (1,)), ((), ())), preferred_element_type=F32)


def _dot_tn(a, b):
    return lax.dot_general(a, b, (((0,), (0,)), ((), ())), preferred_element_type=F32)


def _mod_kernel(v_ref, w_ref, b_ref, o_ref):
    v = v_ref[...]
    sv = v * _sigmoid(v)
    o_ref[0] = jnp.dot(sv, w_ref[0], precision=lax.Precision.HIGHEST,
                       preferred_element_type=F32) + b_ref[0]


def _modulation(c, c_ctx, w_mod, b_mod):
    depth, d, n = w_mod.shape
    b = c.shape[0]
    rows = 8 * pl.cdiv(b + 1, 8)
    v = jnp.zeros((rows, d), F32).at[:b].set(c).at[b].set(c_ctx)
    tn = _largest_tile(n, 1536, LANES)
    out = pl.pallas_call(
        _mod_kernel,
        grid=(depth, n // tn),
        in_specs=[pl.BlockSpec((rows, d), lambda l, j: (0, 0)),
                  pl.BlockSpec((1, d, tn), lambda l, j: (l, 0, j)),
                  pl.BlockSpec((1, 1, tn), lambda l, j: (l, 0, j))],
        out_specs=pl.BlockSpec((1, rows, tn), lambda l, j: (l, 0, j)),
        out_shape=jax.ShapeDtypeStruct((depth, rows, n), F32),
    )(v, w_mod, b_mod.reshape(depth, 1, n))
    return out[:, :b + 1].reshape(depth, b + 1, 6, d)


def _modulated_norm(x, g, mx, mc, row0, lc, shift_row, scale_row):
    tm = x.shape[0]
    xn = x * lax.rsqrt(jnp.mean(x * x, axis=-1, keepdims=True) + EPS) * g
    row = row0 + lax.broadcasted_iota(jnp.int32, (tm, 1), 0)
    is_ctx = row < lc
    a = jnp.where(is_ctx, mc[scale_row:scale_row + 1, :], mx[scale_row:scale_row + 1, :])
    s = jnp.where(is_ctx, mc[shift_row:shift_row + 1, :], mx[shift_row:shift_row + 1, :])
    return xn * (1.0 + a) + s


def _in_proj_kernel(x_ref, mx_ref, mc_ref, g_ref, w_ref, cos_ref, sin_ref, *refs,
                    lc, tm, n_rope, n_first):
    o_ref, o2_ref, h_scr = refs if n_first else (refs[0], None, refs[1])
    i = pl.program_id(1)
    j = pl.program_id(2)

    @pl.when(j == 0)
    def _():
        h = _modulated_norm(x_ref[0], g_ref[...], mx_ref[0], mc_ref[0], i * tm, lc, 0, 1)
        h_scr[...] = h.astype(BF16)

    y = _dot(h_scr[...], w_ref[...])
    if n_rope == 0:
        assert not n_first
        o_ref[0] = y.astype(o_ref.dtype)
    else:
        assert not n_first or n_rope <= n_first

        @pl.when(j < n_rope)
        def _():
            cos = cos_ref[...]
            sin = sin_ref[...]
            for hh in range(y.shape[1] // HEAD_DIM):
                yh = y[:, hh * HEAD_DIM:(hh + 1) * HEAD_DIM]
                o_ref[0, :, hh * HEAD_DIM:(hh + 1) * HEAD_DIM] = (
                    yh * cos + pltpu.roll(yh, HEAD_DIM // 2, 1) * sin).astype(o_ref.dtype)

        @pl.when((j >= n_rope) & (j < n_first) if n_first else j >= n_rope)
        def _():
            o_ref[0] = y.astype(o_ref.dtype)

        if n_first:
            @pl.when(j >= n_first)
            def _():
                o2_ref[0] = y.astype(o2_ref.dtype)


def _in_proj(xa, mod, g, w, cos_t, sin_t, *, lc, tn, n_rope, out_dtype, n_first=0, second_dtype=None):
    b, s, d = xa.shape
    n = w.shape[1]
    tm = _largest_tile(s, ROW_TILE_TARGET, 16)
    if n_first:
        out_specs = [pl.BlockSpec((1, tm, tn), lambda bb, i, j: (bb, i, jnp.minimum(j, n_first - 1))),
                     pl.BlockSpec((1, tm, tn), lambda bb, i, j: (bb, i, jnp.maximum(j - n_first, 0)))]
        out_shape = [jax.ShapeDtypeStruct((b, s, n_first * tn), out_dtype),
                     jax.ShapeDtypeStruct((b, s, n - n_first * tn), second_dtype)]
    else:
        out_specs = pl.BlockSpec((1, tm, tn), lambda bb, i, j: (bb, i, j))
        out_shape = jax.ShapeDtypeStruct((b, s, n), out_dtype)
    return pl.pallas_call(
        functools.partial(_in_proj_kernel, lc=lc, tm=tm, n_rope=n_rope, n_first=n_first),
        grid=(b, s // tm, n // tn),
        in_specs=[pl.BlockSpec((1, tm, d), lambda bb, i, j: (bb, i, 0)),
                  pl.BlockSpec((1, 6, d), lambda bb, i, j: (bb, 0, 0)),
                  pl.BlockSpec((1, 6, d), lambda bb, i, j: (b, 0, 0)),
                  pl.BlockSpec((1, d), lambda bb, i, j: (0, 0)),
                  pl.BlockSpec((d, tn), lambda bb, i, j: (0, j)),
                  pl.BlockSpec((tm, HEAD_DIM), lambda bb, i, j: (i, 0)),
                  pl.BlockSpec((tm, HEAD_DIM), lambda bb, i, j: (i, 0))],
        out_specs=out_specs,
        out_shape=out_shape,
        scratch_shapes=[pltpu.VMEM((tm, d), BF16)],
        compiler_params=pltpu.CompilerParams(
            dimension_semantics=("parallel", "parallel", "arbitrary")),
    )(xa, mod, mod, g.reshape(1, d), w, cos_t, sin_t)


def _out_ffn_kernel(*refs, lc, tm, even, final):
    if even:
        (x_ref, ma_ref, mb_ref, u_ref, mx_ref, mc_ref, g_ref, wo_ref, w1_ref, w3_ref, w2_ref,
         fin_ref, sd_ref, wglu_ref, o_ref, x1_scr, h_scr, acc_scr) = refs
    else:
        (x_ref, ma_ref, mb_ref, mx_ref, mc_ref, g_ref, wo_ref, w1_ref, w3_ref, w2_ref,
         fin_ref, o_ref, x1_scr, h_scr, acc_scr) = refs
    i = pl.program_id(1)
    j = pl.program_id(2)
    half = ma_ref.shape[2]

    def mod_row(r):
        row = i * tm + lax.broadcasted_iota(jnp.int32, (tm, 1), 0)
        return jnp.where(row < lc, mc_ref[0, r:r + 1, :], mx_ref[0, r:r + 1, :])

    @pl.when(j == 0)
    def _():
        if even:
            z = jax.nn.gelu(sd_ref[...] * u_ref[0] + ma_ref[0])
            zb = z.astype(BF16)
            part_a = (z * _sigmoid(_dot(zb, wglu_ref[...]))).astype(BF16)
        else:
            part_a = ma_ref[0].astype(BF16)
        mixed = (_dot(part_a, wo_ref[0:half, :])
                 + _dot(mb_ref[0].astype(BF16), wo_ref[half:2 * half, :]))
        x1 = x_ref[0] + mod_row(2) * mixed
        x1_scr[...] = x1
        h = _modulated_norm(x1, g_ref[...], mx_ref[0], mc_ref[0], i * tm, lc, 3, 4)
        h_scr[...] = h.astype(BF16)
        acc_scr[...] = jnp.zeros_like(acc_scr)

    h = h_scr[...]
    a = _dot(h, w1_ref[...])
    gate = (a * _sigmoid(a)) * _dot(h, w3_ref[...])
    acc_scr[...] += _dot(gate.astype(BF16), w2_ref[...])

    @pl.when(j == pl.num_programs(2) - 1)
    def _():
        x2 = x1_scr[...] + mod_row(5) * acc_scr[...]
        if final:
            x2 = x2 * lax.rsqrt(jnp.mean(x2 * x2, axis=-1, keepdims=True) + EPS) * fin_ref[...]
        o_ref[0] = x2


def _out_ffn(xa, mix_a, mix_b, mod, g, w_out, w1, w3, w2, fin_g, *, lc, final,
             s5_u=None, s5_d=None, w_glu=None):
    b, s, d = xa.shape
    dff = w1.shape[1]
    half = mix_a.shape[2]
    even = s5_u is not None
    tm = _largest_tile(s, ROW_TILE_TARGET, 16)
    tf = _largest_tile(dff, FF_TILE, LANES)
    row_spec = lambda w: pl.BlockSpec((1, tm, w), lambda bb, i, j: (bb, i, 0))
    const2 = lambda shp: pl.BlockSpec(shp, lambda bb, i, j: (0, 0))
    in_specs = [row_spec(d), row_spec(half), row_spec(half)]
    args = [xa, mix_a, mix_b]
    if even:
        in_specs.append(row_spec(half))
        args.append(s5_u)
    in_specs += [pl.BlockSpec((1, 6, d), lambda bb, i, j: (bb, 0, 0)),
                 pl.BlockSpec((1, 6, d), lambda bb, i, j: (b, 0, 0)),
                 const2((1, d)), const2((2 * half, d)),
                 pl.BlockSpec((d, tf), lambda bb, i, j: (0, j)),
                 pl.BlockSpec((d, tf), lambda bb, i, j: (0, j)),
                 pl.BlockSpec((tf, d), lambda bb, i, j: (j, 0)),
                 const2((1, d))]
    args += [mod, mod, g.reshape(1, d), w_out, w1, w3, w2, fin_g.reshape(1, d)]
    if even:
        in_specs += [const2((1, half)), const2((half, half))]
        args += [s5_d.reshape(1, half), w_glu]
    return pl.pallas_call(
        functools.partial(_out_ffn_kernel, lc=lc, tm=tm, even=even, final=final),
        grid=(b, s // tm, dff // tf),
        in_specs=in_specs,
        out_specs=pl.BlockSpec((1, tm, d), lambda bb, i, j: (bb, i, 0)),
        out_shape=jax.ShapeDtypeStruct((b, s, d), F32),
        scratch_shapes=[pltpu.VMEM((tm, d), F32), pltpu.VMEM((tm, d), BF16),
                        pltpu.VMEM((tm, d), F32)],
        compiler_params=pltpu.CompilerParams(
            dimension_semantics=("parallel", "parallel", "arbitrary")),
    )(*args)


def _rope_angles(l, lc, dim):
    rows = l // GRID_W
    r, col = jnp.meshgrid(jnp.arange(rows, dtype=F32), jnp.arange(GRID_W, dtype=F32), indexing='ij')
    quarter = dim // 4
    inv = ROPE_BASE ** (-jnp.arange(quarter, dtype=F32) / quarter)
    ang = jnp.concatenate([r.reshape(-1, 1) * inv, col.reshape(-1, 1) * inv], axis=-1)
    cos = jnp.concatenate([jnp.ones((lc, dim // 2), F32), jnp.cos(ang)], axis=0)
    sin = jnp.concatenate([jnp.zeros((lc, dim // 2), F32), jnp.sin(ang)], axis=0)
    return cos, sin


def _split_pairs_perm(n):
    return np.concatenate([np.arange(0, n, 2), np.arange(1, n, 2)])


def _s5_matrices(a_re, a_im, log_dt, b_re, b_im, c_re, c_im):
    t_len = S5_T
    g, p = a_re.shape[1], a_re.shape[2]
    k = b_re.shape[3]
    tau = jnp.arange(t_len + 1, dtype=F32)
    toep = jnp.zeros((g, t_len, k, t_len, k), F32)
    f_parts, e_parts, a_chunk = [], [], []
    tt = jnp.arange(t_len)
    for r in range(2):
        dt = jnp.exp(log_dt[r])[:, None]
        mag = jnp.exp(a_re[r] * dt)
        ab_re = mag * jnp.cos(a_im[r] * dt)
        ab_im = mag * jnp.sin(a_im[r] * dt)
        nr, ni = ab_re - 1.0, ab_im
        den = a_re[r] * a_re[r] + a_im[r] * a_im[r]
        fr = (nr * a_re[r] + ni * a_im[r]) / den
        fi = (ni * a_re[r] - nr * a_im[r]) / den
        bb_re = fr[..., None] * b_re[r] - fi[..., None] * b_im[r]
        bb_im = fr[..., None] * b_im[r] + fi[..., None] * b_re[r]
        pw_mag = jnp.exp(tau[:, None, None] * (a_re[r] * dt)[None])
        pw_ang = tau[:, None, None] * (a_im[r] * dt)[None]
        pw_re = pw_mag * jnp.cos(pw_ang)
        pw_im = pw_mag * jnp.sin(pw_ang)
        ca_re = c_re[r][None] * pw_re[:, :, None, :] - c_im[r][None] * pw_im[:, :, None, :]
        ca_im = c_re[r][None] * pw_im[:, :, None, :] + c_im[r][None] * pw_re[:, :, None, :]
        m = jnp.sum(ca_re[..., None] * bb_re[None, :, None] - ca_im[..., None] * bb_im[None, :, None],
                    axis=3)
        lag = (tt[None, :] - tt[:, None]) if r == 0 else (tt[:, None] - tt[None, :])
        valid = lag >= 0
        mg = m[jnp.clip(lag, 0, t_len - 1)]
        mg = jnp.where(valid[:, :, None, None, None], mg, 0.0)
        toep = toep + mg.transpose(2, 0, 4, 1, 3)
        pidx = (t_len - 1 - tt) if r == 0 else tt
        f_re = pw_re[pidx][:, :, :, None] * bb_re[None] - pw_im[pidx][:, :, :, None] * bb_im[None]
        f_im = pw_re[pidx][:, :, :, None] * bb_im[None] + pw_im[pidx][:, :, :, None] * bb_re[None]
        f_parts += [f_re.transpose(1, 0, 3, 2).reshape(g, t_len * k, p),
                    f_im.transpose(1, 0, 3, 2).reshape(g, t_len * k, p)]
        qidx = (tt + 1) if r == 0 else (t_len - tt)
        e_parts += [ca_re[qidx].transpose(1, 3, 0, 2).reshape(g, p, t_len * k),
                    (-ca_im[qidx]).transpose(1, 3, 0, 2).reshape(g, p, t_len * k)]
        a_chunk += [pw_re[t_len].reshape(-1), pw_im[t_len].reshape(-1)]
    toep = toep.reshape(g, t_len * k, t_len * k).astype(BF16)
    pad_lo = lambda m_: jnp.concatenate([m_, jnp.zeros_like(m_)], axis=-1)
    pad_hi = lambda m_: jnp.concatenate([jnp.zeros_like(m_), m_], axis=-1)
    odd = (jnp.arange(g) % 2 == 1)[:, None, None]
    f_pad = jnp.stack([jnp.where(odd, pad_hi(m_), pad_lo(m_)) for m_ in f_parts], axis=1)
    pad_lo_r = lambda m_: jnp.concatenate([m_, jnp.zeros_like(m_)], axis=-2)
    pad_hi_r = lambda m_: jnp.concatenate([jnp.zeros_like(m_), m_], axis=-2)
    e_pad = jnp.stack([jnp.where(odd, pad_hi_r(m_), pad_lo_r(m_)) for m_ in e_parts], axis=1)
    return toep, f_pad.astype(BF16), e_pad.astype(BF16), jnp.stack(a_chunk, axis=0)


def _s5_inject_kernel(u_ref, f_ref, o0, o1, o2, o3):
    u0 = u_ref[0]
    u1 = u_ref[1]
    for a, o in enumerate((o0, o1, o2, o3)):
        o[...] = _dot(u0, f_ref[0, a]) + _dot(u1, f_ref[1, a])


def _s5_scan_kernel(s0, s1, s2, s3, a_ref, h0, h1, h2, h3, *, nb, nc, ncc):
    a = a_ref[...]
    arf, aif, arb, aib = a[0:1], a[1:2], a[2:3], a[3:4]
    ct = s0.shape[1]

    def body(i, carry):
        cb = jnp.where(i < ncc, ncc - 1 - i, nc - 1 - (i - ncc))
        new = []
        for bb in range(nb):
            hr, hi, gr, gi = carry[4 * bb:4 * bb + 4]
            rf = bb * nc + i
            rb = bb * nc + cb
            h0[pl.ds(rf, 1), :] = hr
            h1[pl.ds(rf, 1), :] = hi
            h2[pl.ds(rb, 1), :] = gr
            h3[pl.ds(rb, 1), :] = gi
            new += [arf * hr - aif * hi + s0[pl.ds(rf, 1), :],
                    arf * hi + aif * hr + s1[pl.ds(rf, 1), :],
                    arb * gr - aib * gi + s2[pl.ds(rb, 1), :],
                    arb * gi + aib * gr + s3[pl.ds(rb, 1), :]]
        return tuple(new)

    zero = jnp.zeros((1, ct), F32)
    lax.fori_loop(0, nc, body, (zero,) * (4 * nb))


def _s5_readout_kernel(u_ref, t_ref, h0, h1, h2, h3, e_ref, o_ref):
    y = _dot(u_ref[0], t_ref[0])
    for a, h in enumerate((h0, h1, h2, h3)):
        y = y + _dot(h[...].astype(BF16), e_ref[0, a])
    o_ref[0] = y


def _s5_mixer(proj, mats, *, lc):
    toep, f_pad, e_pad, a_chunk = mats
    b, s, _ = proj.shape
    g = toep.shape[0]
    k = S5_GROUP
    width = g * k
    nc = s // S5_T
    ncc = lc // S5_T
    rows = b * nc
    tk = S5_T * k
    u = proj[:, :, :width].reshape(b, nc, S5_T, g, k).transpose(3, 0, 1, 2, 4)
    u = u.reshape(g, rows, tk).astype(BF16)
    sw = g * S5_STATE
    st_shape = jax.ShapeDtypeStruct((rows, sw), F32)
    inj = pl.pallas_call(
        _s5_inject_kernel,
        grid=(g // 2,),
        in_specs=[pl.BlockSpec((2, rows, tk), lambda j: (j, 0, 0)),
                  pl.BlockSpec((2, 4, tk, LANES), lambda j: (j, 0, 0, 0))],
        out_specs=[pl.BlockSpec((rows, LANES), lambda j: (0, j))] * 4,
        out_shape=[st_shape] * 4,
    )(u, f_pad)
    ct = _largest_tile(sw, 512, LANES)
    states = pl.pallas_call(
        functools.partial(_s5_scan_kernel, nb=b, nc=nc, ncc=ncc),
        grid=(sw // ct,),
        in_specs=[pl.BlockSpec((rows, ct), lambda j: (0, j))] * 4
        + [pl.BlockSpec((4, ct), lambda j: (0, j))],
        out_specs=[pl.BlockSpec((rows, ct), lambda j: (0, j))] * 4,
        out_shape=[st_shape] * 4,
    )(*inj, a_chunk)
    y = pl.pallas_call(
        _s5_readout_kernel,
        grid=(g,),
        in_specs=[pl.BlockSpec((1, rows, tk), lambda j: (j, 0, 0)),
                  pl.BlockSpec((1, tk, tk), lambda j: (j, 0, 0))]
        + [pl.BlockSpec((rows, LANES), lambda j: (0, j // 2))] * 4
        + [pl.BlockSpec((1, 4, LANES, tk), lambda j: (j, 0, 0, 0))],
        out_specs=pl.BlockSpec((1, rows, tk), lambda j: (j, 0, 0)),
        out_shape=jax.ShapeDtypeStruct((g, rows, tk), F32),
    )(u, toep, *states, e_pad)
    y = y.reshape(g, b, nc, S5_T, k).transpose(1, 2, 3, 0, 4)
    return y.reshape(b, s, width)


def _mla_weights(w_uq, w_ukv):
    rq = w_uq.shape[0]
    rkv = w_ukv.shape[0]
    hq = w_uq.reshape(rq, MLA_HEADS, MLA_NOPE + MLA_ROPE)
    nope = hq[:, :, :MLA_NOPE]
    r_even = hq[:, :, MLA_NOPE::2]
    r_odd = hq[:, :, MLA_NOPE + 1::2]
    zq = jnp.zeros((rq, MLA_HEADS, HEAD_PAD - MLA_NOPE - MLA_ROPE), F32)
    wq = jnp.concatenate([nope, r_even, r_odd, zq], axis=-1)
    wq_sw = jnp.concatenate([jnp.zeros_like(nope), r_odd, r_even, zq], axis=-1)
    hkv = w_ukv.reshape(rkv, MLA_HEADS, MLA_NOPE + MLA_V)
    zk = jnp.zeros((rkv, MLA_HEADS, HEAD_PAD - MLA_NOPE), F32)
    wk = jnp.concatenate([hkv[:, :, :MLA_NOPE], zk], axis=-1)
    wv = jnp.concatenate([hkv[:, :, MLA_NOPE:], jnp.zeros((rkv, MLA_HEADS, HEAD_PAD - MLA_V), F32)],
                         axis=-1)
    out_in = lambda w: w.transpose(1, 2, 0).astype(BF16)
    return out_in(wq), out_in(wq_sw), wk.transpose(1, 0, 2).astype(BF16), out_in(wv)


def _mla_tables(l, lc):
    cos, sin = _rope_angles(l, lc, MLA_ROPE)
    s = cos.shape[0]
    z_lo = jnp.zeros((s, MLA_NOPE), F32)
    z_hi = jnp.zeros((s, HEAD_PAD - MLA_NOPE - MLA_ROPE), F32)
    ck = jnp.concatenate([z_lo, cos, cos, z_hi], axis=-1)
    sk = jnp.concatenate([z_lo, -sin, sin, z_hi], axis=-1)
    scale = (MLA_NOPE + MLA_ROPE) ** -0.5 * math.log2(math.e)
    cq = jnp.concatenate([jnp.ones_like(z_lo), cos, cos, z_hi], axis=-1) * scale
    return cq.T, sk.T * scale, ck, sk


def _mla_proj_kernel(cq_ref, ckv_ref, kr_ref, krs_ref, qn_ref, kvn_ref, wqt_ref, wqst_ref, wk_ref,
                     wvt_ref, tcq_ref, tsq_ref, tck_ref, tsk_ref, qt_ref, k_ref, vt_ref):
    def norm(v, g):
        return (v * lax.rsqrt(jnp.mean(v * v, axis=-1, keepdims=True) + EPS) * g).astype(BF16)

    hq = norm(cq_ref[0], qn_ref[...])
    hkv = norm(ckv_ref[0], kvn_ref[...])
    k_rope = kr_ref[0] * tck_ref[...] + krs_ref[0] * tsk_ref[...]
    row = lax.broadcasted_iota(jnp.int32, (HEAD_PAD, 1), 0)
    ones_row = (row == MLA_V).astype(F32)
    tcq = tcq_ref[...]
    tsq = tsq_ref[...]
    for hh in range(MLA_HEADS):
        qt = _dot_nt(wqt_ref[hh], hq) * tcq + _dot_nt(wqst_ref[hh], hq) * tsq
        qt_ref[0, hh] = qt.astype(BF16)
        k_ref[0, hh] = (_dot(hkv, wk_ref[hh]) + k_rope).astype(BF16)
        vt_ref[0, hh, 0] = (_dot_nt(wvt_ref[hh], hkv) + ones_row).astype(BF16)


def _attn_kernel(qt_ref, k_ref, vt_ref, o_ref, st_scr, acc_scr, *, lc, tq, tk, s_total):
    heads = qt_ref.shape[1]
    kc = ATT_SCORE_CHUNK
    pc = ATT_PROB_CHUNK
    sub = 8

    def score_chunk(slot, hh, first, c, part):
        row0 = first + c * kc
        if not isinstance(row0, int):
            row0 = pl.multiple_of(row0, kc)
        st = _dot(k_ref[0, hh, pl.ds(row0, kc), :], qt_ref[0, hh])
        st_scr[slot, hh, c * kc:(c + 1) * kc, :] = st
        cm = jnp.max(st.reshape(kc // sub, sub, tq), axis=0)
        return cm if part is None else jnp.maximum(part, cm)

    def prob_chunk(slot, hh, tile, j, m, alpha):
        p = jnp.exp2(st_scr[slot, hh, j * pc:(j + 1) * pc, :] - m).astype(BF16)
        pv = _dot(vt_ref[0, hh, tile, :, j * pc:(j + 1) * pc], p)
        if j == 0:
            acc_scr[hh] = acc_scr[hh] * alpha + pv
        else:
            acc_scr[hh] += pv

    def fold_max(m_old, part):
        m_new = jnp.maximum(m_old, jnp.max(part, axis=0, keepdims=True))
        return m_new, jnp.exp2(m_old - m_new)

    def phase(nkeys, a_args, bc_args):
        parts = [None] * heads
        assert pc // kc == heads and nkeys % pc == 0
        for j in range(nkeys // pc):
            for r in range(heads):
                if a_args is not None:
                    for hh in range(heads):
                        parts[hh] = score_chunk(a_args[0], hh, a_args[1], j * heads + r, parts[hh])
                if bc_args is not None:
                    slot, tile, ms, alphas = bc_args
                    prob_chunk(slot, r, tile, j, ms[r], alphas[r])
        return parts

    def finish():
        out_t = jnp.concatenate(
            [acc_scr[hh, 0:MLA_V, :] / acc_scr[hh, MLA_V:MLA_V + 1, :] for hh in range(heads)],
            axis=0)
        o_ref[0] = out_t.T

    def attend(nkeys, n_tiles):
        acc_scr[...] = jnp.zeros(acc_scr.shape, F32)
        neg = jnp.full((1, tq), -jnp.inf, F32)
        parts = phase(nkeys, (0, 0), None)
        state = []
        for hh in range(heads):
            state += list(fold_max(neg, parts[hh]))

        def step(t, slot, carry):
            ms, alphas = carry[0::2], carry[1::2]
            nxt = phase(nkeys, (1 - slot, (t + 1) * nkeys), (slot, t, ms, alphas))
            new = []
            for hh in range(heads):
                new += list(fold_max(ms[hh], nxt[hh]))
            return tuple(new)

        pairs = (n_tiles - 1) // 2
        state = tuple(state)
        if pairs > 0:
            state = lax.fori_loop(
                0, pairs, lambda i, carry: step(2 * i + 1, 1, step(2 * i, 0, carry)), state)
        for t in range(2 * pairs, n_tiles - 1):
            state = step(t, t % 2, state)
        phase(nkeys, None, ((n_tiles - 1) % 2, n_tiles - 1, state[0::2], state[1::2]))
        finish()

    is_ctx = pl.program_id(2) < lc // tq

    @pl.when(is_ctx)
    def _():
        attend(lc, 1)

    @pl.when(jnp.logical_not(is_ctx))
    def _():
        attend(tk, s_total // tk)


def _mla_mixer(proj, q_norm, kv_norm, weights, tables, *, lc):
    b, s, _ = proj.shape
    wqt, wqst, wk, wvt = weights
    tq = ATT_TQ
    tk = _largest_tile(s, ATT_TK, ATT_PROB_CHUNK)
    assert lc % tq == 0 and s % tq == 0 and lc <= tk and lc % ATT_PROB_CHUNK == 0
    rq = wqt.shape[2]
    rkv = wk.shape[1]
    base = 512
    head_shape = jax.ShapeDtypeStruct((b, MLA_HEADS, s, HEAD_PAD), BF16)
    full = lambda shp: pl.BlockSpec(shp, lambda bb, i: tuple(0 for _ in shp))
    tab = pl.BlockSpec((tk, HEAD_PAD), lambda bb, i: (i, 0))
    tab_t = pl.BlockSpec((HEAD_PAD, tk), lambda bb, i: (0, i))
    head_rows = pl.BlockSpec((1, MLA_HEADS, tk, HEAD_PAD), lambda bb, i: (bb, 0, i, 0))
    qt, k, vt = pl.pallas_call(
        _mla_proj_kernel,
        grid=(b, s // tk),
        in_specs=[pl.BlockSpec((1, tk, rq), lambda bb, i: (bb, i, base // rq)),
                  pl.BlockSpec((1, tk, rkv), lambda bb, i: (bb, i, (base + rq) // rkv)),
                  pl.BlockSpec((1, tk, LANES), lambda bb, i: (bb, i, (base + rq + rkv) // LANES)),
                  pl.BlockSpec((1, tk, LANES), lambda bb, i: (bb, i, (base + rq + rkv) // LANES + 1)),
                  full((1, rq)), full((1, rkv)),
                  full(wqt.shape), full(wqst.shape), full(wk.shape), full(wvt.shape),
                  tab_t, tab_t, tab, tab],
        out_specs=[pl.BlockSpec((1, MLA_HEADS, HEAD_PAD, tk), lambda bb, i: (bb, 0, 0, i)), head_rows,
                   pl.BlockSpec((1, MLA_HEADS, 1, HEAD_PAD, tk), lambda bb, i: (bb, 0, i, 0, 0))],
        out_shape=[jax.ShapeDtypeStruct((b, MLA_HEADS, HEAD_PAD, s), BF16), head_shape,
                   jax.ShapeDtypeStruct((b, MLA_HEADS, s // tk, HEAD_PAD, tk), BF16)],
    )(proj, proj, proj, proj, q_norm.reshape(1, rq), kv_norm.reshape(1, rkv),
      wqt, wqst, wk, wvt, *tables)
    return pl.pallas_call(
        functools.partial(_attn_kernel, lc=lc, tq=tq, tk=tk, s_total=s),
        grid=(b, MLA_HEADS // 2, s // tq),
        in_specs=[pl.BlockSpec((1, 2, HEAD_PAD, tq), lambda bb, hp, i: (bb, hp, 0, i)),
                  pl.BlockSpec((1, 2, s, HEAD_PAD), lambda bb, hp, i: (bb, hp, 0, 0)),
                  pl.BlockSpec((1, 2, s // tk, HEAD_PAD, tk), lambda bb, hp, i: (bb, hp, 0, 0, 0))],
        out_specs=pl.BlockSpec((1, tq, 2 * MLA_V), lambda bb, hp, i: (bb, i, hp)),
        out_shape=jax.ShapeDtypeStruct((b, s, MLA_HEADS * MLA_V), F32),
        scratch_shapes=[pltpu.VMEM((2, 2, tk, tq), F32), pltpu.VMEM((2, HEAD_PAD, tq), F32)],
        compiler_params=pltpu.CompilerParams(
            dimension_semantics=("parallel", "parallel", "arbitrary")),
    )(qt, k, vt)


def _state_scan_kernel(kv_ref, dec_ref, st_ref, *, nch, ncc):
    hd = HEAD_DIM

    def body(i, carry):
        sf, sb = carry
        cb = jnp.where(i < ncc, ncc - 1 - i, nch - 1 - (i - ncc))
        st_ref[0, 0, i, :, 0:hd] = sf.astype(BF16)
        st_ref[0, 0, cb, :, hd:2 * hd] = sb.astype(BF16)
        sf = dec_ref[0, 0, i, :, 0:hd] * sf + kv_ref[0, 0, i, :, 0:hd]
        sb = dec_ref[0, 0, cb, :, hd:2 * hd] * sb + kv_ref[0, 0, cb, :, hd:2 * hd]
        return sf, sb

    zero = jnp.zeros((hd, hd), F32)
    lax.fori_loop(0, nch, body, (zero, zero))


def _state_scan(kv, dec, *, ncc):
    b, h, nch = kv.shape[:3]
    hd = HEAD_DIM
    return pl.pallas_call(
        functools.partial(_state_scan_kernel, nch=nch, ncc=ncc),
        grid=(b, h),
        in_specs=[pl.BlockSpec((1, 1, nch, hd, 2 * hd), lambda bb, hh: (bb, hh, 0, 0, 0)),
                  pl.BlockSpec((1, 1, nch, 1, 2 * hd), lambda bb, hh: (bb, hh, 0, 0, 0))],
        out_specs=pl.BlockSpec((1, 1, nch, hd, 2 * hd), lambda bb, hh: (bb, hh, 0, 0, 0)),
        out_shape=jax.ShapeDtypeStruct((b, h, nch, hd, 2 * hd), BF16),
    )(kv, dec)


def _readout_kernel(oi_ref, qd_ref, st_ref, g_ref, gn_ref, o_ref, *, cpt, center):
    def body(c, _):
        r0 = pl.multiple_of(c * CHUNK, CHUNK)
        o = oi_ref[0, 0, pl.ds(r0, CHUNK), :] + _dot_nt(qd_ref[0, 0, pl.ds(r0, CHUNK), :],
                                                        st_ref[0, 0, c])
        if center:
            o = o - jnp.mean(o, axis=-1, keepdims=True)
        o = o * lax.rsqrt(jnp.mean(o * o, axis=-1, keepdims=True) + EPS) * gn_ref[...]
        gate = g_ref[0, pl.ds(r0, CHUNK), :].astype(F32)
        o_ref[0, pl.ds(r0, CHUNK), :] = o * (gate * _sigmoid(gate))
        return 0

    lax.fori_loop(0, cpt, body, 0, unroll=True)


def _readout(o_intra, q_dec, states, proj, gate_col, gn, *, center):
    b, h, s, hd = o_intra.shape
    nch = s // CHUNK
    cpt = _largest_tile(nch, 11, 1)
    tl = cpt * CHUNK
    return pl.pallas_call(
        functools.partial(_readout_kernel, cpt=cpt, center=center),
        grid=(b, h, nch // cpt),
        in_specs=[pl.BlockSpec((1, 1, tl, hd), lambda bb, hh, t: (bb, hh, t, 0)),
                  pl.BlockSpec((1, 1, tl, 2 * hd), lambda bb, hh, t: (bb, hh, t, 0)),
                  pl.BlockSpec((1, 1, cpt, hd, 2 * hd), lambda bb, hh, t: (bb, hh, t, 0, 0)),
                  pl.BlockSpec((1, tl, hd), lambda bb, hh, t: (bb, t, gate_col + hh)),
                  pl.BlockSpec((1, hd), lambda bb, hh, t: (0, hh))],
        out_specs=pl.BlockSpec((1, tl, hd), lambda bb, hh, t: (bb, t, hh)),
        out_shape=jax.ShapeDtypeStruct((b, s, h * hd), F32),
    )(o_intra, q_dec, states, proj, gn.reshape(1, h * hd))


def _local_out_shapes(b, h, s):
    nch = s // CHUNK
    hd = HEAD_DIM
    return [jax.ShapeDtypeStruct((b, h, s, hd), F32),
            jax.ShapeDtypeStruct((b, h, s, 2 * hd), BF16),
            jax.ShapeDtypeStruct((b, h, nch, hd, 2 * hd), F32)]


def _local_out_specs(cpt):
    tl = cpt * CHUNK
    hd = HEAD_DIM
    return [pl.BlockSpec((1, 1, tl, hd), lambda bb, hh, t: (bb, hh, t, 0)),
            pl.BlockSpec((1, 1, tl, 2 * hd), lambda bb, hh, t: (bb, hh, t, 0)),
            pl.BlockSpec((1, 1, cpt, hd, 2 * hd), lambda bb, hh, t: (bb, hh, t, 0, 0))]


def _ret_constants():
    c = CHUNK
    pos = np.arange(c, dtype=np.float64)
    rel = pos[:, None] - pos[None, :]
    scale = HEAD_DIM ** -0.5
    dmask = np.zeros((RET_HEADS, c, c))
    qw = np.zeros((RET_HEADS, c, 2 * HEAD_DIM))
    kw = np.zeros((RET_HEADS, c, 2 * HEAD_DIM))
    dec = np.zeros((RET_HEADS, 1, 2 * HEAD_DIM))
    for r, offset in enumerate(RET_DECAY_OFFSETS):
        lg = np.log1p(-np.exp2(-(5.0 + offset) - np.arange(RET_HEADS, dtype=np.float64)))[:, None, None]
        dist = rel if r == 0 else -rel
        dmask += np.where(dist >= 0, np.exp(lg * np.maximum(dist, 0.0)), 0.0) * scale
        steps_in = (pos + 1) if r == 0 else (c - pos)
        steps_out = (c - 1 - pos) if r == 0 else pos
        sl = slice(r * HEAD_DIM, (r + 1) * HEAD_DIM)
        qw[:, :, sl] = np.exp(lg[:, :, 0] * steps_in[None, :])[:, :, None]
        kw[:, :, sl] = np.exp(lg[:, :, 0] * steps_out[None, :])[:, :, None] * scale
        dec[:, :, sl] = np.exp(lg * c)
    return (jnp.asarray(dmask, F32), jnp.asarray(qw, F32), jnp.asarray(kw, F32), jnp.asarray(dec, F32))


def _ret_local_kernel(q_ref, k_ref, v_ref, dm_ref, qw_ref, kw_ref, oi_ref, qd_ref, kv_ref, *, cpt):
    dm = dm_ref[0]
    qw = qw_ref[0]
    kw = kw_ref[0]

    def body(c, _):
        r0 = pl.multiple_of(c * CHUNK, CHUNK)
        q = q_ref[0, pl.ds(r0, CHUNK), :].astype(F32)
        k = k_ref[0, pl.ds(r0, CHUNK), :].astype(F32)
        v = v_ref[0, pl.ds(r0, CHUNK), :].astype(BF16)
        inner = _dot_nt(q.astype(BF16), k.astype(BF16)) * dm
        oi_ref[0, 0, pl.ds(r0, CHUNK), :] = _dot(inner.astype(BF16), v)
        q2 = jnp.concatenate([q, q], axis=-1) * qw
        qd_ref[0, 0, pl.ds(r0, CHUNK), :] = q2.astype(BF16)
        k2 = (jnp.concatenate([k, k], axis=-1) * kw).astype(BF16)
        kv_ref[0, 0, c] = _dot_tn(v, k2)
        return 0

    lax.fori_loop(0, cpt, body, 0, unroll=True)


def _retention_mixer(proj, gn, consts, *, lc):
    b, s, _ = proj.shape
    h = RET_HEADS
    hd = HEAD_DIM
    dmask, qw, kw, dec = consts
    nch = s // CHUNK
    cpt = _largest_tile(nch, 11, 1)
    tl = cpt * CHUNK
    col = lambda base: pl.BlockSpec((1, tl, hd), lambda bb, hh, t: (bb, t, base + hh))
    per_head = lambda w: pl.BlockSpec((1, CHUNK, w), lambda bb, hh, t: (hh, 0, 0))
    o_intra, q_dec, kv = pl.pallas_call(
        functools.partial(_ret_local_kernel, cpt=cpt),
        grid=(b, h, nch // cpt),
        in_specs=[col(0), col(h), col(2 * h), per_head(CHUNK), per_head(2 * hd), per_head(2 * hd)],
        out_specs=_local_out_specs(cpt),
        out_shape=_local_out_shapes(b, h, s),
    )(proj, proj, proj, dmask, qw, kw)
    dec_all = jnp.broadcast_to(dec[None, :, None], (b, h, nch, 1, 2 * hd))
    states = _state_scan(kv, dec_all, ncc=lc // CHUNK)
    return _readout(o_intra, q_dec, states, proj, 3 * h, gn, center=True)


HG_LEVELS = (1, 2, 4, 8, 16, 32, 64)


def _hg_constants():
    c = CHUNK
    t = np.arange(c)[:, None]
    s = np.arange(c)[None, :]
    sums, masks = [], []
    for blk in HG_LEVELS:
        same = (t // blk) == (s // blk)
        right = (t // blk) % 2 == 1
        sums.append((right & same & (s <= t)) | (~right & same & (s > t)))
        masks.append(((t // (2 * blk)) == (s // (2 * blk))) & right & ((s // blk) % 2 == 0))
    sums.append(np.broadcast_to(s <= t, (c, c)))
    sums.append(np.broadcast_to(s > t, (c, c)))
    if HG_BLOCK > 1:
        sums.append(((t // HG_BLOCK) == (s // HG_BLOCK)) & (s <= t))
    g_f = np.concatenate([m.astype(np.float32) for m in sums], axis=0)
    m_f = np.stack([m.astype(np.float32) for m in masks], axis=0)
    flip = lambda a: a[..., ::-1, ::-1]
    g_b = np.concatenate([flip(m.astype(np.float32)) for m in sums], axis=0)
    m_b = np.stack([flip(m.astype(np.float32)) for m in masks], axis=0)
    return (jnp.asarray(np.stack([g_f, g_b]), BF16), jnp.asarray(np.stack([m_f, m_b]), F32))


def _hg_local_kernel(q_ref, ff_ref, fb_ref, v_ref, lb_ref, gs_ref, mk_ref, oi_ref, qd_ref, kv_ref,
                     dec_ref, *, cpt):
    c_len = CHUNK
    nlev = len(HG_LEVELS)
    lb = lb_ref[...]
    row = lax.broadcasted_iota(jnp.int32, (c_len, c_len), 0)
    col = lax.broadcasted_iota(jnp.int32, (c_len, c_len), 1)
    row_in_blk = row % HG_BLOCK

    def body(c, _):
        r0 = pl.multiple_of(c * c_len, c_len)
        q = q_ref[0, pl.ds(r0, c_len), :].astype(F32)
        v = v_ref[0, pl.ds(r0, c_len), :].astype(BF16)
        att = jnp.zeros((c_len, c_len), F32)
        q_parts, kv_parts, dec_parts = [], [], []
        for d, f_ref in enumerate((ff_ref, fb_ref)):
            f = lb + (1.0 - lb) * _sigmoid(f_ref[0, pl.ds(r0, c_len), :].astype(F32))
            kk = 1.0 - f
            z = _dot(gs_ref[d], jnp.log(f).astype(BF16))
            for lv in range(nlev):
                e = jnp.exp(z[lv * c_len:(lv + 1) * c_len])
                att = att + mk_ref[d, lv] * _dot_nt((q * e).astype(BF16), (kk * e).astype(BF16))
            zq = z[nlev * c_len:(nlev + 1) * c_len]
            zk = z[(nlev + 1) * c_len:(nlev + 2) * c_len]
            cl = z[(nlev + 2) * c_len:(nlev + 3) * c_len] if HG_BLOCK > 1 else None
            for delta in range(HG_BLOCK):
                if delta == 0:
                    prod = q * kk
                else:
                    sh = delta if d == 0 else c_len - delta
                    prod = (q * pltpu.roll(kk, sh, 0)
                            * jnp.exp(jnp.minimum(cl - pltpu.roll(cl, sh, 0), 0.0)))
                a = jnp.sum(prod, axis=-1, keepdims=True)
                if d == 0:
                    hit = (col == row - delta) & (row_in_blk >= delta)
                else:
                    hit = (col == row + delta) & (row_in_blk + delta < HG_BLOCK)
                att = att + jnp.where(hit, a, 0.0)
            q_parts.append(q * jnp.exp(zq))
            kv_parts.append(kk * jnp.exp(zk))
            last = c_len - 1 if d == 0 else 0
            dec_parts.append(jnp.exp(zq[last:last + 1]))
        oi_ref[0, 0, pl.ds(r0, c_len), :] = _dot(att.astype(BF16), v)
        qd_ref[0, 0, pl.ds(r0, c_len), :] = jnp.concatenate(q_parts, axis=-1).astype(BF16)
        kv_ref[0, 0, c] = _dot_tn(v, jnp.concatenate(kv_parts, axis=-1).astype(BF16))
        dec_ref[0, 0, c] = jnp.concatenate(dec_parts, axis=-1)
        return 0

    lax.fori_loop(0, cpt, body, 0, unroll=2)


def _hgrn2_mixer(proj, lb, gn, consts, *, lc, col0):
    b, s, _ = proj.shape
    h = HG_HEADS
    hd = HEAD_DIM
    gsum, masks = consts
    nch = s // CHUNK
    cpt = _largest_tile(nch, 11, 1)
    tl = cpt * CHUNK
    col = lambda base: pl.BlockSpec((1, tl, hd), lambda bb, hh, t: (bb, t, col0 + base + hh))
    full = lambda a: pl.BlockSpec(a.shape, lambda bb, hh, t: tuple(0 for _ in a.shape))
    o_intra, q_dec, kv, dec = pl.pallas_call(
        functools.partial(_hg_local_kernel, cpt=cpt),
        grid=(b, h, nch // cpt),
        in_specs=[col(0), col(h), col(2 * h), col(3 * h),
                  pl.BlockSpec((1, hd), lambda bb, hh, t: (0, hh)), full(gsum), full(masks)],
        out_specs=_local_out_specs(cpt)
        + [pl.BlockSpec((1, 1, cpt, 1, 2 * hd), lambda bb, hh, t: (bb, hh, t, 0, 0))],
        out_shape=_local_out_shapes(b, h, s)
        + [jax.ShapeDtypeStruct((b, h, nch, 1, 2 * hd), F32)],
    )(proj, proj, proj, proj, lb.reshape(1, h * hd), gsum, masks)
    states = _state_scan(kv, dec, ncc=lc // CHUNK)
    return _readout(o_intra, q_dec, states, proj, col0 + 4 * h, gn, center=False)


def kernel(x, c, ctx, c_ctx, w_mod, b_mod, norm1_g, norm2_g, ffn_w1, ffn_w3, ffn_w2, w_in_even, w_out_even, s5_a_re, s5_a_im, s5_log_dt, s5_b_re, s5_b_im, s5_c_re, s5_c_im, s5_d, s5_w_glu, mla_q_norm, mla_w_uq, mla_kv_norm, mla_w_ukv, w_in_odd, w_out_odd, ret_gn, hg_lb_logits, hg_gn, final_norm):
    b, l, d = x.shape
    lc = ctx.shape[1]
    depth = w_mod.shape[0]
    n_odd = w_in_odd.shape[0]
    assert lc % CHUNK == 0 and l % CHUNK == 0 and l % GRID_W == 0

    xa = jnp.concatenate([ctx, x], axis=1)
    mods = _modulation(c, c_ctx, w_mod, b_mod)

    cos_r, sin_r = _rope_angles(l, lc, HEAD_DIM)
    ret_cos = jnp.concatenate([cos_r, cos_r], axis=-1)
    ret_sin = jnp.concatenate([-sin_r, sin_r], axis=-1)
    mla_tabs = _mla_tables(l, lc)
    ret_consts = _ret_constants()
    hg_consts = _hg_constants()
    hg_lb = jnp.cumsum(jax.nn.softmax(hg_lb_logits.astype(F32), axis=0), axis=0)[:n_odd]

    s5_w = s5_d.shape[1]
    kr0 = w_in_even.shape[2] - MLA_ROPE
    kr_e = w_in_even[:, :, kr0::2]
    kr_o = w_in_even[:, :, kr0 + 1::2]
    z_lo = jnp.zeros(w_in_even.shape[:2] + (MLA_NOPE,), F32)
    z_hi = jnp.zeros(w_in_even.shape[:2] + (HEAD_PAD - MLA_NOPE - MLA_ROPE,), F32)
    w_in_e = jnp.concatenate([w_in_even[:, :, :kr0], z_lo, kr_e, kr_o, z_hi, z_lo, kr_o, kr_e, z_hi],
                             axis=-1).astype(BF16)
    qk_w = 2 * RET_HEADS * HEAD_DIM
    perm = (np.arange(qk_w) // HEAD_DIM) * HEAD_DIM + np.tile(_split_pairs_perm(HEAD_DIM),
                                                               qk_w // HEAD_DIM)
    w_in_o = jnp.concatenate([w_in_odd[:, :, perm], w_in_odd[:, :, qk_w:]], axis=-1).astype(BF16)

    s5_mats = jax.vmap(_s5_matrices)(s5_a_re, s5_a_im, s5_log_dt, s5_b_re, s5_b_im, s5_c_re, s5_c_im)
    mla_w = jax.vmap(_mla_weights)(mla_w_uq, mla_w_ukv)

    w1 = ffn_w1.astype(BF16)
    w3 = ffn_w3.astype(BF16)
    w2 = ffn_w2.astype(BF16)
    w_out_e = w_out_even.astype(BF16)
    w_out_o = w_out_odd.astype(BF16)
    w_glu = s5_w_glu.astype(BF16)

    for li in range(depth):
        j = li // 2
        final = li == depth - 1
        if li % 2 == 0:
            proj = _in_proj(xa, mods[li], norm1_g[li], w_in_e[j], ret_cos, ret_sin,
                            lc=lc, tn=w_in_e.shape[2], n_rope=0, out_dtype=F32)
            y_s5 = _s5_mixer(proj, tuple(m[j] for m in s5_mats), lc=lc)
            attn = _mla_mixer(proj, mla_q_norm[j], mla_kv_norm[j],
                              tuple(w[j] for w in mla_w), mla_tabs, lc=lc)
            xa = _out_ffn(xa, y_s5, attn, mods[li], norm2_g[li], w_out_e[j], w1[li], w3[li], w2[li],
                          final_norm, lc=lc, final=final, s5_u=proj, s5_d=s5_d[j], w_glu=w_glu[j])
        else:
            proj_ret, proj_hg = _in_proj(xa, mods[li], norm1_g[li], w_in_o[j], ret_cos, ret_sin,
                                         lc=lc, tn=RET_HEADS * HEAD_DIM, n_rope=2, out_dtype=BF16,
                                         n_first=4, second_dtype=F32)
            ret = _retention_mixer(proj_ret, ret_gn[j], ret_consts, lc=lc)
            hg = _hgrn2_mixer(proj_hg, hg_lb[j], hg_gn[j], hg_consts, lc=lc, col0=0)
            xa = _out_ffn(xa, ret, hg, mods[li], norm2_g[li], w_out_o[j], w1[li], w3[li], w2[li],
                          final_norm, lc=lc, final=final)
    return xa[:, lc:, :]
```

```python
import functools
import math

import jax
import jax.numpy as jnp
import numpy as np
from jax import lax
from jax.experimental import pallas as pl
from jax.experimental.pallas import tpu as pltpu

F32 = jnp.float32
BF16 = jnp.bfloat16

EPS = 1e-6
ROPE_BASE = 10000.0
GRID_W = 64

S5_GROUP = 16
S5_STATE = 64
S5_T = 16

MLA_HEADS = 8
MLA_NOPE = 64
MLA_ROPE = 32
MLA_V = 64
HEAD_PAD = 128

RET_HEADS = 4
HG_HEADS = 4
HEAD_DIM = 128
CHUNK = 128
RET_DECAY_OFFSETS = (0.0, 0.5)
HG_BLOCK = 1

LANES = 128
ROW_TILE_TARGET = 1056
FF_TILE = 256
ATT_TQ = 256
ATT_TK = 768
ATT_SCORE_CHUNK = 128
ATT_PROB_CHUNK = 256


def _largest_tile(n, target, mult):
    best = None
    for t in range(mult, min(n, target) + 1, mult):
        if n % t == 0:
            best = t
    assert best is not None, (n, target, mult)
    return best


def _sigmoid(v):
    return 1.0 / (1.0 + jnp.exp(-v))


def _dot(a, b):
    return jnp.dot(a, b, preferred_element_type=F32)


def _dot_nt(a, b):
    return lax.dot_general(a, b, (((1,), (1,)), ((), ())), preferred_element_type=F32)


def _dot_tn(a, b):
    return lax.dot_general(a, b, (((0,), (0,)), ((), ())), preferred_element_type=F32)


def _mod_kernel(v_ref, w_ref, b_ref, o_ref):
    v = v_ref[...]
    sv = v * _sigmoid(v)
    o_ref[0] = jnp.dot(sv, w_ref[0], precision=lax.Precision.HIGHEST,
                       preferred_element_type=F32) + b_ref[0]


def _modulation(c, c_ctx, w_mod, b_mod):
    depth, d, n = w_mod.shape
    b = c.shape[0]
    rows = 8 * pl.cdiv(b + 1, 8)
    v = jnp.zeros((rows, d), F32).at[:b].set(c).at[b].set(c_ctx)
    tn = _largest_tile(n, 1536, LANES)
    out = pl.pallas_call(
        _mod_kernel,
        grid=(depth, n // tn),
        in_specs=[pl.BlockSpec((rows, d), lambda l, j: (0, 0)),
                  pl.BlockSpec((1, d, tn), lambda l, j: (l, 0, j)),
                  pl.BlockSpec((1, 1, tn), lambda l, j: (l, 0, j))],
        out_specs=pl.BlockSpec((1, rows, tn), lambda l, j: (l, 0, j)),
        out_shape=jax.ShapeDtypeStruct((depth, rows, n), F32),
    )(v, w_mod, b_mod.reshape(depth, 1, n))
    return out[:, :b + 1].reshape(depth, b + 1, 6, d)


def _modulated_norm(x, g, mx, mc, row0, lc, shift_row, scale_row):
    tm = x.shape[0]
    xn = x * lax.rsqrt(jnp.mean(x * x, axis=-1, keepdims=True) + EPS) * g
    row = row0 + lax.broadcasted_iota(jnp.int32, (tm, 1), 0)
    is_ctx = row < lc
    a = jnp.where(is_ctx, mc[scale_row:scale_row + 1, :], mx[scale_row:scale_row + 1, :])
    s = jnp.where(is_ctx, mc[shift_row:shift_row + 1, :], mx[shift_row:shift_row + 1, :])
    return xn * (1.0 + a) + s


def _in_proj_kernel(x_ref, mx_ref, mc_ref, g_ref, w_ref, cos_ref, sin_ref, o_ref, h_scr,
                    *, lc, tm, n_rope):
    i = pl.program_id(1)
    j = pl.program_id(2)

    @pl.when(j == 0)
    def _():
        h = _modulated_norm(x_ref[0], g_ref[...], mx_ref[0], mc_ref[0], i * tm, lc, 0, 1)
        h_scr[...] = h.astype(BF16)

    y = _dot(h_scr[...], w_ref[...])
    if n_rope == 0:
        o_ref[0] = y.astype(o_ref.dtype)
    else:
        @pl.when(j < n_rope)
        def _():
            cos = cos_ref[...]
            sin = sin_ref[...]
            for hh in range(y.shape[1] // HEAD_DIM):
                yh = y[:, hh * HEAD_DIM:(hh + 1) * HEAD_DIM]
                o_ref[0, :, hh * HEAD_DIM:(hh + 1) * HEAD_DIM] = (
                    yh * cos + pltpu.roll(yh, HEAD_DIM // 2, 1) * sin).astype(o_ref.dtype)

        @pl.when(j >= n_rope)
        def _():
            o_ref[0] = y.astype(o_ref.dtype)


def _in_proj(xa, mod, g, w, cos_t, sin_t, *, lc, tn, n_rope, out_dtype):
    b, s, d = xa.shape
    n = w.shape[1]
    tm = _largest_tile(s, ROW_TILE_TARGET, 16)
    return pl.pallas_call(
        functools.partial(_in_proj_kernel, lc=lc, tm=tm, n_rope=n_rope),
        grid=(b, s // tm, n // tn),
        in_specs=[pl.BlockSpec((1, tm, d), lambda bb, i, j: (bb, i, 0)),
                  pl.BlockSpec((1, 6, d), lambda bb, i, j: (bb, 0, 0)),
                  pl.BlockSpec((1, 6, d), lambda bb, i, j: (b, 0, 0)),
                  pl.BlockSpec((1, d), lambda bb, i, j: (0, 0)),
                  pl.BlockSpec((d, tn), lambda bb, i, j: (0, j)),
                  pl.BlockSpec((tm, HEAD_DIM), lambda bb, i, j: (i, 0)),
                  pl.BlockSpec((tm, HEAD_DIM), lambda bb, i, j: (i, 0))],
        out_specs=pl.BlockSpec((1, tm, tn), lambda bb, i, j: (bb, i, j)),
        out_shape=jax.ShapeDtypeStruct((b, s, n), out_dtype),
        scratch_shapes=[pltpu.VMEM((tm, d), BF16)],
        compiler_params=pltpu.CompilerParams(
            dimension_semantics=("parallel", "parallel", "arbitrary")),
    )(xa, mod, mod, g.reshape(1, d), w, cos_t, sin_t)


def _out_ffn_kernel(*refs, lc, tm, even, final):
    if even:
        (x_ref, ma_ref, mb_ref, u_ref, mx_ref, mc_ref, g_ref, wo_ref, w1_ref, w3_ref, w2_ref,
         fin_ref, sd_ref, wglu_ref, o_ref, x1_scr, h_scr, acc_scr) = refs
    else:
        (x_ref, ma_ref, mb_ref, mx_ref, mc_ref, g_ref, wo_ref, w1_ref, w3_ref, w2_ref,
         fin_ref, o_ref, x1_scr, h_scr, acc_scr) = refs
    i = pl.program_id(1)
    j = pl.program_id(2)
    half = ma_ref.shape[2]

    def mod_row(r):
        row = i * tm + lax.broadcasted_iota(jnp.int32, (tm, 1), 0)
        return jnp.where(row < lc, mc_ref[0, r:r + 1, :], mx_ref[0, r:r + 1, :])

    @pl.when(j == 0)
    def _():
        if even:
            z = jax.nn.gelu(sd_ref[...] * u_ref[0] + ma_ref[0])
            zb = z.astype(BF16)
            part_a = (z * _sigmoid(_dot(zb, wglu_ref[...]))).astype(BF16)
        else:
            part_a = ma_ref[0].astype(BF16)
        mixed = (_dot(part_a, wo_ref[0:half, :])
                 + _dot(mb_ref[0].astype(BF16), wo_ref[half:2 * half, :]))
        x1 = x_ref[0] + mod_row(2) * mixed
        x1_scr[...] = x1
        h = _modulated_norm(x1, g_ref[...], mx_ref[0], mc_ref[0], i * tm, lc, 3, 4)
        h_scr[...] = h.astype(BF16)
        acc_scr[...] = jnp.zeros_like(acc_scr)

    h = h_scr[...]
    a = _dot(h, w1_ref[...])
    gate = (a * _sigmoid(a)) * _dot(h, w3_ref[...])
    acc_scr[...] += _dot(gate.astype(BF16), w2_ref[...])

    @pl.when(j == pl.num_programs(2) - 1)
    def _():
        x2 = x1_scr[...] + mod_row(5) * acc_scr[...]
        if final:
            x2 = x2 * lax.rsqrt(jnp.mean(x2 * x2, axis=-1, keepdims=True) + EPS) * fin_ref[...]
        o_ref[0] = x2


def _out_ffn(xa, mix_a, mix_b, mod, g, w_out, w1, w3, w2, fin_g, *, lc, final,
             s5_u=None, s5_d=None, w_glu=None):
    b, s, d = xa.shape
    dff = w1.shape[1]
    half = mix_a.shape[2]
    even = s5_u is not None
    tm = _largest_tile(s, ROW_TILE_TARGET, 16)
    tf = _largest_tile(dff, FF_TILE, LANES)
    row_spec = lambda w: pl.BlockSpec((1, tm, w), lambda bb, i, j: (bb, i, 0))
    const2 = lambda shp: pl.BlockSpec(shp, lambda bb, i, j: (0, 0))
    in_specs = [row_spec(d), row_spec(half), row_spec(half)]
    args = [xa, mix_a, mix_b]
    if even:
        in_specs.append(row_spec(half))
        args.append(s5_u)
    in_specs += [pl.BlockSpec((1, 6, d), lambda bb, i, j: (bb, 0, 0)),
                 pl.BlockSpec((1, 6, d), lambda bb, i, j: (b, 0, 0)),
                 const2((1, d)), const2((2 * half, d)),
                 pl.BlockSpec((d, tf), lambda bb, i, j: (0, j)),
                 pl.BlockSpec((d, tf), lambda bb, i, j: (0, j)),
                 pl.BlockSpec((tf, d), lambda bb, i, j: (j, 0)),
                 const2((1, d))]
    args += [mod, mod, g.reshape(1, d), w_out, w1, w3, w2, fin_g.reshape(1, d)]
    if even:
        in_specs += [const2((1, half)), const2((half, half))]
        args += [s5_d.reshape(1, half), w_glu]
    return pl.pallas_call(
        functools.partial(_out_ffn_kernel, lc=lc, tm=tm, even=even, final=final),
        grid=(b, s // tm, dff // tf),
        in_specs=in_specs,
        out_specs=pl.BlockSpec((1, tm, d), lambda bb, i, j: (bb, i, 0)),
        out_shape=jax.ShapeDtypeStruct((b, s, d), F32),
        scratch_shapes=[pltpu.VMEM((tm, d), F32), pltpu.VMEM((tm, d), BF16),
                        pltpu.VMEM((tm, d), F32)],
        compiler_params=pltpu.CompilerParams(
            dimension_semantics=("parallel", "parallel", "arbitrary")),
    )(*args)


def _rope_angles(l, lc, dim):
    rows = l // GRID_W
    r, col = jnp.meshgrid(jnp.arange(rows, dtype=F32), jnp.arange(GRID_W, dtype=F32), indexing='ij')
    quarter = dim // 4
    inv = ROPE_BASE ** (-jnp.arange(quarter, dtype=F32) / quarter)
    ang = jnp.concatenate([r.reshape(-1, 1) * inv, col.reshape(-1, 1) * inv], axis=-1)
    cos = jnp.concatenate([jnp.ones((lc, dim // 2), F32), jnp.cos(ang)], axis=0)
    sin = jnp.concatenate([jnp.zeros((lc, dim // 2), F32), jnp.sin(ang)], axis=0)
    return cos, sin


def _split_pairs_perm(n):
    return np.concatenate([np.arange(0, n, 2), np.arange(1, n, 2)])


def _s5_matrices(a_re, a_im, log_dt, b_re, b_im, c_re, c_im):
    t_len = S5_T
    g, p = a_re.shape[1], a_re.shape[2]
    k = b_re.shape[3]
    tau = jnp.arange(t_len + 1, dtype=F32)
    toep = jnp.zeros((g, t_len, k, t_len, k), F32)
    f_parts, e_parts, a_chunk = [], [], []
    tt = jnp.arange(t_len)
    for r in range(2):
        dt = jnp.exp(log_dt[r])[:, None]
        mag = jnp.exp(a_re[r] * dt)
        ab_re = mag * jnp.cos(a_im[r] * dt)
        ab_im = mag * jnp.sin(a_im[r] * dt)
        nr, ni = ab_re - 1.0, ab_im
        den = a_re[r] * a_re[r] + a_im[r] * a_im[r]
        fr = (nr * a_re[r] + ni * a_im[r]) / den
        fi = (ni * a_re[r] - nr * a_im[r]) / den
        bb_re = fr[..., None] * b_re[r] - fi[..., None] * b_im[r]
        bb_im = fr[..., None] * b_im[r] + fi[..., None] * b_re[r]
        pw_mag = jnp.exp(tau[:, None, None] * (a_re[r] * dt)[None])
        pw_ang = tau[:, None, None] * (a_im[r] * dt)[None]
        pw_re = pw_mag * jnp.cos(pw_ang)
        pw_im = pw_mag * jnp.sin(pw_ang)
        ca_re = c_re[r][None] * pw_re[:, :, None, :] - c_im[r][None] * pw_im[:, :, None, :]
        ca_im = c_re[r][None] * pw_im[:, :, None, :] + c_im[r][None] * pw_re[:, :, None, :]
        m = jnp.sum(ca_re[..., None] * bb_re[None, :, None] - ca_im[..., None] * bb_im[None, :, None],
                    axis=3)
        lag = (tt[None, :] - tt[:, None]) if r == 0 else (tt[:, None] - tt[None, :])
        valid = lag >= 0
        mg = m[jnp.clip(lag, 0, t_len - 1)]
        mg = jnp.where(valid[:, :, None, None, None], mg, 0.0)
        toep = toep + mg.transpose(2, 0, 4, 1, 3)
        pidx = (t_len - 1 - tt) if r == 0 else tt
        f_re = pw_re[pidx][:, :, :, None] * bb_re[None] - pw_im[pidx][:, :, :, None] * bb_im[None]
        f_im = pw_re[pidx][:, :, :, None] * bb_im[None] + pw_im[pidx][:, :, :, None] * bb_re[None]
        f_parts += [f_re.transpose(1, 0, 3, 2).reshape(g, t_len * k, p),
                    f_im.transpose(1, 0, 3, 2).reshape(g, t_len * k, p)]
        qidx = (tt + 1) if r == 0 else (t_len - tt)
        e_parts += [ca_re[qidx].transpose(1, 3, 0, 2).reshape(g, p, t_len * k),
                    (-ca_im[qidx]).transpose(1, 3, 0, 2).reshape(g, p, t_len * k)]
        a_chunk += [pw_re[t_len].reshape(-1), pw_im[t_len].reshape(-1)]
    toep = toep.reshape(g, t_len * k, t_len * k).astype(BF16)
    pad_lo = lambda m_: jnp.concatenate([m_, jnp.zeros_like(m_)], axis=-1)
    pad_hi = lambda m_: jnp.concatenate([jnp.zeros_like(m_), m_], axis=-1)
    odd = (jnp.arange(g) % 2 == 1)[:, None, None]
    f_pad = jnp.stack([jnp.where(odd, pad_hi(m_), pad_lo(m_)) for m_ in f_parts], axis=1)
    pad_lo_r = lambda m_: jnp.concatenate([m_, jnp.zeros_like(m_)], axis=-2)
    pad_hi_r = lambda m_: jnp.concatenate([jnp.zeros_like(m_), m_], axis=-2)
    e_pad = jnp.stack([jnp.where(odd, pad_hi_r(m_), pad_lo_r(m_)) for m_ in e_parts], axis=1)
    return toep, f_pad.astype(BF16), e_pad.astype(BF16), jnp.stack(a_chunk, axis=0)


def _s5_inject_kernel(u_ref, f_ref, o0, o1, o2, o3):
    u0 = u_ref[0]
    u1 = u_ref[1]
    for a, o in enumerate((o0, o1, o2, o3)):
        o[...] = _dot(u0, f_ref[0, a]) + _dot(u1, f_ref[1, a])


def _s5_scan_kernel(s0, s1, s2, s3, a_ref, h0, h1, h2, h3, *, nb, nc, ncc):
    a = a_ref[...]
    arf, aif, arb, aib = a[0:1], a[1:2], a[2:3], a[3:4]
    ct = s0.shape[1]

    def body(i, carry):
        cb = jnp.where(i < ncc, ncc - 1 - i, nc - 1 - (i - ncc))
        new = []
        for bb in range(nb):
            hr, hi, gr, gi = carry[4 * bb:4 * bb + 4]
            rf = bb * nc + i
            rb = bb * nc + cb
            h0[pl.ds(rf, 1), :] = hr
            h1[pl.ds(rf, 1), :] = hi
            h2[pl.ds(rb, 1), :] = gr
            h3[pl.ds(rb, 1), :] = gi
            new += [arf * hr - aif * hi + s0[pl.ds(rf, 1), :],
                    arf * hi + aif * hr + s1[pl.ds(rf, 1), :],
                    arb * gr - aib * gi + s2[pl.ds(rb, 1), :],
                    arb * gi + aib * gr + s3[pl.ds(rb, 1), :]]
        return tuple(new)

    zero = jnp.zeros((1, ct), F32)
    lax.fori_loop(0, nc, body, (zero,) * (4 * nb))


def _s5_readout_kernel(u_ref, t_ref, h0, h1, h2, h3, e_ref, o_ref):
    y = _dot(u_ref[0], t_ref[0])
    for a, h in enumerate((h0, h1, h2, h3)):
        y = y + _dot(h[...].astype(BF16), e_ref[0, a])
    o_ref[0] = y


def _s5_mixer(proj, mats, *, lc):
    toep, f_pad, e_pad, a_chunk = mats
    b, s, _ = proj.shape
    g = toep.shape[0]
    k = S5_GROUP
    width = g * k
    nc = s // S5_T
    ncc = lc // S5_T
    rows = b * nc
    tk = S5_T * k
    u = proj[:, :, :width].reshape(b, nc, S5_T, g, k).transpose(3, 0, 1, 2, 4)
    u = u.reshape(g, rows, tk).astype(BF16)
    sw = g * S5_STATE
    st_shape = jax.ShapeDtypeStruct((rows, sw), F32)
    inj = pl.pallas_call(
        _s5_inject_kernel,
        grid=(g // 2,),
        in_specs=[pl.BlockSpec((2, rows, tk), lambda j: (j, 0, 0)),
                  pl.BlockSpec((2, 4, tk, LANES), lambda j: (j, 0, 0, 0))],
        out_specs=[pl.BlockSpec((rows, LANES), lambda j: (0, j))] * 4,
        out_shape=[st_shape] * 4,
    )(u, f_pad)
    ct = _largest_tile(sw, 512, LANES)
    states = pl.pallas_call(
        functools.partial(_s5_scan_kernel, nb=b, nc=nc, ncc=ncc),
        grid=(sw // ct,),
        in_specs=[pl.BlockSpec((rows, ct), lambda j: (0, j))] * 4
        + [pl.BlockSpec((4, ct), lambda j: (0, j))],
        out_specs=[pl.BlockSpec((rows, ct), lambda j: (0, j))] * 4,
        out_shape=[st_shape] * 4,
    )(*inj, a_chunk)
    y = pl.pallas_call(
        _s5_readout_kernel,
        grid=(g,),
        in_specs=[pl.BlockSpec((1, rows, tk), lambda j: (j, 0, 0)),
                  pl.BlockSpec((1, tk, tk), lambda j: (j, 0, 0))]
        + [pl.BlockSpec((rows, LANES), lambda j: (0, j // 2))] * 4
        + [pl.BlockSpec((1, 4, LANES, tk), lambda j: (j, 0, 0, 0))],
        out_specs=pl.BlockSpec((1, rows, tk), lambda j: (j, 0, 0)),
        out_shape=jax.ShapeDtypeStruct((g, rows, tk), F32),
    )(u, toep, *states, e_pad)
    y = y.reshape(g, b, nc, S5_T, k).transpose(1, 2, 3, 0, 4)
    return y.reshape(b, s, width)


def _mla_weights(w_uq, w_ukv):
    rq = w_uq.shape[0]
    rkv = w_ukv.shape[0]
    hq = w_uq.reshape(rq, MLA_HEADS, MLA_NOPE + MLA_ROPE)
    nope = hq[:, :, :MLA_NOPE]
    r_even = hq[:, :, MLA_NOPE::2]
    r_odd = hq[:, :, MLA_NOPE + 1::2]
    zq = jnp.zeros((rq, MLA_HEADS, HEAD_PAD - MLA_NOPE - MLA_ROPE), F32)
    wq = jnp.concatenate([nope, r_even, r_odd, zq], axis=-1)
    wq_sw = jnp.concatenate([jnp.zeros_like(nope), r_odd, r_even, zq], axis=-1)
    hkv = w_ukv.reshape(rkv, MLA_HEADS, MLA_NOPE + MLA_V)
    zk = jnp.zeros((rkv, MLA_HEADS, HEAD_PAD - MLA_NOPE), F32)
    wk = jnp.concatenate([hkv[:, :, :MLA_NOPE], zk], axis=-1)
    wv = jnp.concatenate([hkv[:, :, MLA_NOPE:], jnp.zeros((rkv, MLA_HEADS, HEAD_PAD - MLA_V), F32)],
                         axis=-1)
    out_in = lambda w: w.transpose(1, 2, 0).astype(BF16)
    return out_in(wq), out_in(wq_sw), wk.transpose(1, 0, 2).astype(BF16), out_in(wv)


def _mla_tables(l, lc):
    cos, sin = _rope_angles(l, lc, MLA_ROPE)
    s = cos.shape[0]
    z_lo = jnp.zeros((s, MLA_NOPE), F32)
    z_hi = jnp.zeros((s, HEAD_PAD - MLA_NOPE - MLA_ROPE), F32)
    ck = jnp.concatenate([z_lo, cos, cos, z_hi], axis=-1)
    sk = jnp.concatenate([z_lo, -sin, sin, z_hi], axis=-1)
    scale = (MLA_NOPE + MLA_ROPE) ** -0.5 * math.log2(math.e)
    cq = jnp.concatenate([jnp.ones_like(z_lo), cos, cos, z_hi], axis=-1) * scale
    return cq.T, sk.T * scale, ck, sk


def _mla_proj_kernel(cq_ref, ckv_ref, kr_ref, krs_ref, qn_ref, kvn_ref, wqt_ref, wqst_ref, wk_ref,
                     wvt_ref, tcq_ref, tsq_ref, tck_ref, tsk_ref, qt_ref, k_ref, vt_ref):
    def norm(v, g):
        return (v * lax.rsqrt(jnp.mean(v * v, axis=-1, keepdims=True) + EPS) * g).astype(BF16)

    hq = norm(cq_ref[0], qn_ref[...])
    hkv = norm(ckv_ref[0], kvn_ref[...])
    k_rope = kr_ref[0] * tck_ref[...] + krs_ref[0] * tsk_ref[...]
    row = lax.broadcasted_iota(jnp.int32, (HEAD_PAD, 1), 0)
    ones_row = (row == MLA_V).astype(F32)
    tcq = tcq_ref[...]
    tsq = tsq_ref[...]
    for hh in range(MLA_HEADS):
        qt = _dot_nt(wqt_ref[hh], hq) * tcq + _dot_nt(wqst_ref[hh], hq) * tsq
        qt_ref[0, hh] = qt.astype(BF16)
        k_ref[0, hh] = (_dot(hkv, wk_ref[hh]) + k_rope).astype(BF16)
        vt_ref[0, hh, 0] = (_dot_nt(wvt_ref[hh], hkv) + ones_row).astype(BF16)


def _attn_kernel(qt_ref, k_ref, vt_ref, o_ref, st_scr, acc_scr, *, lc, tq, tk, s_total):
    heads = qt_ref.shape[1]
    kc = ATT_SCORE_CHUNK
    pc = ATT_PROB_CHUNK
    sub = 8

    def score_chunk(slot, hh, first, c, part):
        row0 = first + c * kc
        if not isinstance(row0, int):
            row0 = pl.multiple_of(row0, kc)
        st = _dot(k_ref[0, hh, pl.ds(row0, kc), :], qt_ref[0, hh])
        st_scr[slot, hh, c * kc:(c + 1) * kc, :] = st
        cm = jnp.max(st.reshape(kc // sub, sub, tq), axis=0)
        return cm if part is None else jnp.maximum(part, cm)

    def prob_chunk(slot, hh, tile, j, m, alpha):
        p = jnp.exp2(st_scr[slot, hh, j * pc:(j + 1) * pc, :] - m).astype(BF16)
        pv = _dot(vt_ref[0, hh, tile, :, j * pc:(j + 1) * pc], p)
        if j == 0:
            acc_scr[hh] = acc_scr[hh] * alpha + pv
        else:
            acc_scr[hh] += pv

    def fold_max(m_old, part):
        m_new = jnp.maximum(m_old, jnp.max(part, axis=0, keepdims=True))
        return m_new, jnp.exp2(m_old - m_new)

    def phase(nkeys, a_args, bc_args):
        parts = [None] * heads
        assert pc // kc == heads and nkeys % pc == 0
        for j in range(nkeys // pc):
            for r in range(heads):
                if a_args is not None:
                    for hh in range(heads):
                        parts[hh] = score_chunk(a_args[0], hh, a_args[1], j * heads + r, parts[hh])
                if bc_args is not None:
                    slot, tile, ms, alphas = bc_args
                    prob_chunk(slot, r, tile, j, ms[r], alphas[r])
        return parts

    def finish():
        out_t = jnp.concatenate(
            [acc_scr[hh, 0:MLA_V, :] / acc_scr[hh, MLA_V:MLA_V + 1, :] for hh in range(heads)],
            axis=0)
        o_ref[0] = out_t.T

    def attend(nkeys, n_tiles):
        acc_scr[...] = jnp.zeros(acc_scr.shape, F32)
        neg = jnp.full((1, tq), -jnp.inf, F32)
        parts = phase(nkeys, (0, 0), None)
        state = []
        for hh in range(heads):
            state += list(fold_max(neg, parts[hh]))

        def step(t, slot, carry):
            ms, alphas = carry[0::2], carry[1::2]
            nxt = phase(nkeys, (1 - slot, (t + 1) * nkeys), (slot, t, ms, alphas))
            new = []
            for hh in range(heads):
                new += list(fold_max(ms[hh], nxt[hh]))
            return tuple(new)

        pairs = (n_tiles - 1) // 2
        state = tuple(state)
        if pairs > 0:
            state = lax.fori_loop(
                0, pairs, lambda i, carry: step(2 * i + 1, 1, step(2 * i, 0, carry)), state)
        for t in range(2 * pairs, n_tiles - 1):
            state = step(t, t % 2, state)
        phase(nkeys, None, ((n_tiles - 1) % 2, n_tiles - 1, state[0::2], state[1::2]))
        finish()

    is_ctx = pl.program_id(2) < lc // tq

    @pl.when(is_ctx)
    def _():
        attend(lc, 1)

    @pl.when(jnp.logical_not(is_ctx))
    def _():
        attend(tk, s_total // tk)


def _mla_mixer(proj, q_norm, kv_norm, weights, tables, *, lc):
    b, s, _ = proj.shape
    wqt, wqst, wk, wvt = weights
    tq = ATT_TQ
    tk = _largest_tile(s, ATT_TK, ATT_PROB_CHUNK)
    assert lc % tq == 0 and s % tq == 0 and lc <= tk and lc % ATT_PROB_CHUNK == 0
    rq = wqt.shape[2]
    rkv = wk.shape[1]
    base = 512
    head_shape = jax.ShapeDtypeStruct((b, MLA_HEADS, s, HEAD_PAD), BF16)
    full = lambda shp: pl.BlockSpec(shp, lambda bb, i: tuple(0 for _ in shp))
    tab = pl.BlockSpec((tk, HEAD_PAD), lambda bb, i: (i, 0))
    tab_t = pl.BlockSpec((HEAD_PAD, tk), lambda bb, i: (0, i))
    head_rows = pl.BlockSpec((1, MLA_HEADS, tk, HEAD_PAD), lambda bb, i: (bb, 0, i, 0))
    qt, k, vt = pl.pallas_call(
        _mla_proj_kernel,
        grid=(b, s // tk),
        in_specs=[pl.BlockSpec((1, tk, rq), lambda bb, i: (bb, i, base // rq)),
                  pl.BlockSpec((1, tk, rkv), lambda bb, i: (bb, i, (base + rq) // rkv)),
                  pl.BlockSpec((1, tk, LANES), lambda bb, i: (bb, i, (base + rq + rkv) // LANES)),
                  pl.BlockSpec((1, tk, LANES), lambda bb, i: (bb, i, (base + rq + rkv) // LANES + 1)),
                  full((1, rq)), full((1, rkv)),
                  full(wqt.shape), full(wqst.shape), full(wk.shape), full(wvt.shape),
                  tab_t, tab_t, tab, tab],
        out_specs=[pl.BlockSpec((1, MLA_HEADS, HEAD_PAD, tk), lambda bb, i: (bb, 0, 0, i)), head_rows,
                   pl.BlockSpec((1, MLA_HEADS, 1, HEAD_PAD, tk), lambda bb, i: (bb, 0, i, 0, 0))],
        out_shape=[jax.ShapeDtypeStruct((b, MLA_HEADS, HEAD_PAD, s), BF16), head_shape,
                   jax.ShapeDtypeStruct((b, MLA_HEADS, s // tk, HEAD_PAD, tk), BF16)],
    )(proj, proj, proj, proj, q_norm.reshape(1, rq), kv_norm.reshape(1, rkv),
      wqt, wqst, wk, wvt, *tables)
    return pl.pallas_call(
        functools.partial(_attn_kernel, lc=lc, tq=tq, tk=tk, s_total=s),
        grid=(b, MLA_HEADS // 2, s // tq),
        in_specs=[pl.BlockSpec((1, 2, HEAD_PAD, tq), lambda bb, hp, i: (bb, hp, 0, i)),
                  pl.BlockSpec((1, 2, s, HEAD_PAD), lambda bb, hp, i: (bb, hp, 0, 0)),
                  pl.BlockSpec((1, 2, s // tk, HEAD_PAD, tk), lambda bb, hp, i: (bb, hp, 0, 0, 0))],
        out_specs=pl.BlockSpec((1, tq, 2 * MLA_V), lambda bb, hp, i: (bb, i, hp)),
        out_shape=jax.ShapeDtypeStruct((b, s, MLA_HEADS * MLA_V), F32),
        scratch_shapes=[pltpu.VMEM((2, 2, tk, tq), F32), pltpu.VMEM((2, HEAD_PAD, tq), F32)],
        compiler_params=pltpu.CompilerParams(
            dimension_semantics=("parallel", "parallel", "arbitrary")),
    )(qt, k, vt)


def _state_scan_kernel(kv_ref, dec_ref, st_ref, *, nch, ncc):
    hd = HEAD_DIM

    def body(i, carry):
        sf, sb = carry
        cb = jnp.where(i < ncc, ncc - 1 - i, nch - 1 - (i - ncc))
        st_ref[0, 0, i, :, 0:hd] = sf.astype(BF16)
        st_ref[0, 0, cb, :, hd:2 * hd] = sb.astype(BF16)
        sf = dec_ref[0, 0, i, :, 0:hd] * sf + kv_ref[0, 0, i, :, 0:hd]
        sb = dec_ref[0, 0, cb, :, hd:2 * hd] * sb + kv_ref[0, 0, cb, :, hd:2 * hd]
        return sf, sb

    zero = jnp.zeros((hd, hd), F32)
    lax.fori_loop(0, nch, body, (zero, zero))


def _state_scan(kv, dec, *, ncc):
    b, h, nch = kv.shape[:3]
    hd = HEAD_DIM
    return pl.pallas_call(
        functools.partial(_state_scan_kernel, nch=nch, ncc=ncc),
        grid=(b, h),
        in_specs=[pl.BlockSpec((1, 1, nch, hd, 2 * hd), lambda bb, hh: (bb, hh, 0, 0, 0)),
                  pl.BlockSpec((1, 1, nch, 1, 2 * hd), lambda bb, hh: (bb, hh, 0, 0, 0))],
        out_specs=pl.BlockSpec((1, 1, nch, hd, 2 * hd), lambda bb, hh: (bb, hh, 0, 0, 0)),
        out_shape=jax.ShapeDtypeStruct((b, h, nch, hd, 2 * hd), BF16),
    )(kv, dec)


def _readout_kernel(oi_ref, qd_ref, st_ref, g_ref, gn_ref, o_ref, *, cpt, center):
    def body(c, _):
        r0 = pl.multiple_of(c * CHUNK, CHUNK)
        o = oi_ref[0, 0, pl.ds(r0, CHUNK), :] + _dot_nt(qd_ref[0, 0, pl.ds(r0, CHUNK), :],
                                                        st_ref[0, 0, c])
        if center:
            o = o - jnp.mean(o, axis=-1, keepdims=True)
        o = o * lax.rsqrt(jnp.mean(o * o, axis=-1, keepdims=True) + EPS) * gn_ref[...]
        gate = g_ref[0, pl.ds(r0, CHUNK), :].astype(F32)
        o_ref[0, pl.ds(r0, CHUNK), :] = o * (gate * _sigmoid(gate))
        return 0

    lax.fori_loop(0, cpt, body, 0, unroll=True)


def _readout(o_intra, q_dec, states, proj, gate_col, gn, *, center):
    b, h, s, hd = o_intra.shape
    nch = s // CHUNK
    cpt = _largest_tile(nch, 11, 1)
    tl = cpt * CHUNK
    return pl.pallas_call(
        functools.partial(_readout_kernel, cpt=cpt, center=center),
        grid=(b, h, nch // cpt),
        in_specs=[pl.BlockSpec((1, 1, tl, hd), lambda bb, hh, t: (bb, hh, t, 0)),
                  pl.BlockSpec((1, 1, tl, 2 * hd), lambda bb, hh, t: (bb, hh, t, 0)),
                  pl.BlockSpec((1, 1, cpt, hd, 2 * hd), lambda bb, hh, t: (bb, hh, t, 0, 0)),
                  pl.BlockSpec((1, tl, hd), lambda bb, hh, t: (bb, t, gate_col + hh)),
                  pl.BlockSpec((1, hd), lambda bb, hh, t: (0, hh))],
        out_specs=pl.BlockSpec((1, tl, hd), lambda bb, hh, t: (bb, t, hh)),
        out_shape=jax.ShapeDtypeStruct((b, s, h * hd), F32),
    )(o_intra, q_dec, states, proj, gn.reshape(1, h * hd))


def _local_out_shapes(b, h, s):
    nch = s // CHUNK
    hd = HEAD_DIM
    return [jax.ShapeDtypeStruct((b, h, s, hd), F32),
            jax.ShapeDtypeStruct((b, h, s, 2 * hd), BF16),
            jax.ShapeDtypeStruct((b, h, nch, hd, 2 * hd), F32)]


def _local_out_specs(cpt):
    tl = cpt * CHUNK
    hd = HEAD_DIM
    return [pl.BlockSpec((1, 1, tl, hd), lambda bb, hh, t: (bb, hh, t, 0)),
            pl.BlockSpec((1, 1, tl, 2 * hd), lambda bb, hh, t: (bb, hh, t, 0)),
            pl.BlockSpec((1, 1, cpt, hd, 2 * hd), lambda bb, hh, t: (bb, hh, t, 0, 0))]


def _ret_constants():
    c = CHUNK
    pos = np.arange(c, dtype=np.float64)
    rel = pos[:, None] - pos[None, :]
    scale = HEAD_DIM ** -0.5
    dmask = np.zeros((RET_HEADS, c, c))
    qw = np.zeros((RET_HEADS, c, 2 * HEAD_DIM))
    kw = np.zeros((RET_HEADS, c, 2 * HEAD_DIM))
    dec = np.zeros((RET_HEADS, 1, 2 * HEAD_DIM))
    for r, offset in enumerate(RET_DECAY_OFFSETS):
        lg = np.log1p(-np.exp2(-(5.0 + offset) - np.arange(RET_HEADS, dtype=np.float64)))[:, None, None]
        dist = rel if r == 0 else -rel
        dmask += np.where(dist >= 0, np.exp(lg * np.maximum(dist, 0.0)), 0.0) * scale
        steps_in = (pos + 1) if r == 0 else (c - pos)
        steps_out = (c - 1 - pos) if r == 0 else pos
        sl = slice(r * HEAD_DIM, (r + 1) * HEAD_DIM)
        qw[:, :, sl] = np.exp(lg[:, :, 0] * steps_in[None, :])[:, :, None]
        kw[:, :, sl] = np.exp(lg[:, :, 0] * steps_out[None, :])[:, :, None] * scale
        dec[:, :, sl] = np.exp(lg * c)
    return (jnp.asarray(dmask, F32), jnp.asarray(qw, F32), jnp.asarray(kw, F32), jnp.asarray(dec, F32))


def _ret_local_kernel(q_ref, k_ref, v_ref, dm_ref, qw_ref, kw_ref, oi_ref, qd_ref, kv_ref, *, cpt):
    dm = dm_ref[0]
    qw = qw_ref[0]
    kw = kw_ref[0]

    def body(c, _):
        r0 = pl.multiple_of(c * CHUNK, CHUNK)
        q = q_ref[0, pl.ds(r0, CHUNK), :].astype(F32)
        k = k_ref[0, pl.ds(r0, CHUNK), :].astype(F32)
        v = v_ref[0, pl.ds(r0, CHUNK), :].astype(BF16)
        inner = _dot_nt(q.astype(BF16), k.astype(BF16)) * dm
        oi_ref[0, 0, pl.ds(r0, CHUNK), :] = _dot(inner.astype(BF16), v)
        q2 = jnp.concatenate([q, q], axis=-1) * qw
        qd_ref[0, 0, pl.ds(r0, CHUNK), :] = q2.astype(BF16)
        k2 = (jnp.concatenate([k, k], axis=-1) * kw).astype(BF16)
        kv_ref[0, 0, c] = _dot_tn(v, k2)
        return 0

    lax.fori_loop(0, cpt, body, 0, unroll=True)


def _retention_mixer(proj, gn, consts, *, lc):
    b, s, _ = proj.shape
    h = RET_HEADS
    hd = HEAD_DIM
    dmask, qw, kw, dec = consts
    nch = s // CHUNK
    cpt = _largest_tile(nch, 11, 1)
    tl = cpt * CHUNK
    col = lambda base: pl.BlockSpec((1, tl, hd), lambda bb, hh, t: (bb, t, base + hh))
    per_head = lambda w: pl.BlockSpec((1, CHUNK, w), lambda bb, hh, t: (hh, 0, 0))
    o_intra, q_dec, kv = pl.pallas_call(
        functools.partial(_ret_local_kernel, cpt=cpt),
        grid=(b, h, nch // cpt),
        in_specs=[col(0), col(h), col(2 * h), per_head(CHUNK), per_head(2 * hd), per_head(2 * hd)],
        out_specs=_local_out_specs(cpt),
        out_shape=_local_out_shapes(b, h, s),
    )(proj, proj, proj, dmask, qw, kw)
    dec_all = jnp.broadcast_to(dec[None, :, None], (b, h, nch, 1, 2 * hd))
    states = _state_scan(kv, dec_all, ncc=lc // CHUNK)
    return _readout(o_intra, q_dec, states, proj, 3 * h, gn, center=True)


HG_LEVELS = (1, 2, 4, 8, 16, 32, 64)


def _hg_constants():
    c = CHUNK
    t = np.arange(c)[:, None]
    s = np.arange(c)[None, :]
    sums, masks = [], []
    for blk in HG_LEVELS:
        same = (t // blk) == (s // blk)
        right = (t // blk) % 2 == 1
        sums.append((right & same & (s <= t)) | (~right & same & (s > t)))
        masks.append(((t // (2 * blk)) == (s // (2 * blk))) & right & ((s // blk) % 2 == 0))
    sums.append(np.broadcast_to(s <= t, (c, c)))
    sums.append(np.broadcast_to(s > t, (c, c)))
    sums.append(((t // HG_BLOCK) == (s // HG_BLOCK)) & (s <= t))
    g_f = np.concatenate([m.astype(np.float32) for m in sums], axis=0)
    m_f = np.stack([m.astype(np.float32) for m in masks], axis=0)
    flip = lambda a: a[..., ::-1, ::-1]
    g_b = np.concatenate([flip(m.astype(np.float32)) for m in sums], axis=0)
    m_b = np.stack([flip(m.astype(np.float32)) for m in masks], axis=0)
    return (jnp.asarray(np.stack([g_f, g_b]), BF16), jnp.asarray(np.stack([m_f, m_b]), F32))


def _hg_local_kernel(q_ref, ff_ref, fb_ref, v_ref, lb_ref, gs_ref, mk_ref, oi_ref, qd_ref, kv_ref,
                     dec_ref, *, cpt):
    c_len = CHUNK
    nlev = len(HG_LEVELS)
    lb = lb_ref[...]
    row = lax.broadcasted_iota(jnp.int32, (c_len, c_len), 0)
    col = lax.broadcasted_iota(jnp.int32, (c_len, c_len), 1)
    row_in_blk = row % HG_BLOCK

    def body(c, _):
        r0 = pl.multiple_of(c * c_len, c_len)
        q = q_ref[0, pl.ds(r0, c_len), :].astype(F32)
        v = v_ref[0, pl.ds(r0, c_len), :].astype(BF16)
        att = jnp.zeros((c_len, c_len), F32)
        q_parts, kv_parts, dec_parts = [], [], []
        for d, f_ref in enumerate((ff_ref, fb_ref)):
            f = lb + (1.0 - lb) * _sigmoid(f_ref[0, pl.ds(r0, c_len), :].astype(F32))
            kk = 1.0 - f
            z = _dot(gs_ref[d], jnp.log(f).astype(BF16))
            for lv in range(nlev):
                e = jnp.exp(z[lv * c_len:(lv + 1) * c_len])
                att = att + mk_ref[d, lv] * _dot_nt((q * e).astype(BF16), (kk * e).astype(BF16))
            zq = z[nlev * c_len:(nlev + 1) * c_len]
            zk = z[(nlev + 1) * c_len:(nlev + 2) * c_len]
            cl = z[(nlev + 2) * c_len:(nlev + 3) * c_len]
            for delta in range(HG_BLOCK):
                if delta == 0:
                    prod = q * kk
                else:
                    sh = delta if d == 0 else c_len - delta
                    prod = (q * pltpu.roll(kk, sh, 0)
                            * jnp.exp(jnp.minimum(cl - pltpu.roll(cl, sh, 0), 0.0)))
                a = jnp.sum(prod, axis=-1, keepdims=True)
                if d == 0:
                    hit = (col == row - delta) & (row_in_blk >= delta)
                else:
                    hit = (col == row + delta) & (row_in_blk + delta < HG_BLOCK)
                att = att + jnp.where(hit, a, 0.0)
            q_parts.append(q * jnp.exp(zq))
            kv_parts.append(kk * jnp.exp(zk))
            last = c_len - 1 if d == 0 else 0
            dec_parts.append(jnp.exp(zq[last:last + 1]))
        oi_ref[0, 0, pl.ds(r0, c_len), :] = _dot(att.astype(BF16), v)
        qd_ref[0, 0, pl.ds(r0, c_len), :] = jnp.concatenate(q_parts, axis=-1).astype(BF16)
        kv_ref[0, 0, c] = _dot_tn(v, jnp.concatenate(kv_parts, axis=-1).astype(BF16))
        dec_ref[0, 0, c] = jnp.concatenate(dec_parts, axis=-1)
        return 0

    lax.fori_loop(0, cpt, body, 0, unroll=2)


def _hgrn2_mixer(proj, lb, gn, consts, *, lc, col0):
    b, s, _ = proj.shape
    h = HG_HEADS
    hd = HEAD_DIM
    gsum, masks = consts
    nch = s // CHUNK
    cpt = _largest_tile(nch, 11, 1)
    tl = cpt * CHUNK
    col = lambda base: pl.BlockSpec((1, tl, hd), lambda bb, hh, t: (bb, t, col0 + base + hh))
    full = lambda a: pl.BlockSpec(a.shape, lambda bb, hh, t: tuple(0 for _ in a.shape))
    o_intra, q_dec, kv, dec = pl.pallas_call(
        functools.partial(_hg_local_kernel, cpt=cpt),
        grid=(b, h, nch // cpt),
        in_specs=[col(0), col(h), col(2 * h), col(3 * h),
                  pl.BlockSpec((1, hd), lambda bb, hh, t: (0, hh)), full(gsum), full(masks)],
        out_specs=_local_out_specs(cpt)
        + [pl.BlockSpec((1, 1, cpt, 1, 2 * hd), lambda bb, hh, t: (bb, hh, t, 0, 0))],
        out_shape=_local_out_shapes(b, h, s)
        + [jax.ShapeDtypeStruct((b, h, nch, 1, 2 * hd), F32)],
    )(proj, proj, proj, proj, lb.reshape(1, h * hd), gsum, masks)
    states = _state_scan(kv, dec, ncc=lc // CHUNK)
    return _readout(o_intra, q_dec, states, proj, col0 + 4 * h, gn, center=False)


def kernel(x, c, ctx, c_ctx, w_mod, b_mod, norm1_g, norm2_g, ffn_w1, ffn_w3, ffn_w2, w_in_even, w_out_even, s5_a_re, s5_a_im, s5_log_dt, s5_b_re, s5_b_im, s5_c_re, s5_c_im, s5_d, s5_w_glu, mla_q_norm, mla_w_uq, mla_kv_norm, mla_w_ukv, w_in_odd, w_out_odd, ret_gn, hg_lb_logits, hg_gn, final_norm):
    b, l, d = x.shape
    lc = ctx.shape[1]
    depth = w_mod.shape[0]
    n_odd = w_in_odd.shape[0]
    assert lc % CHUNK == 0 and l % CHUNK == 0 and l % GRID_W == 0

    xa = jnp.concatenate([ctx, x], axis=1)
    mods = _modulation(c, c_ctx, w_mod, b_mod)

    cos_r, sin_r = _rope_angles(l, lc, HEAD_DIM)
    ret_cos = jnp.concatenate([cos_r, cos_r], axis=-1)
    ret_sin = jnp.concatenate([-sin_r, sin_r], axis=-1)
    mla_tabs = _mla_tables(l, lc)
    ret_consts = _ret_constants()
    hg_consts = _hg_constants()
    hg_lb = jnp.cumsum(jax.nn.softmax(hg_lb_logits.astype(F32), axis=0), axis=0)[:n_odd]

    s5_w = s5_d.shape[1]
    kr0 = w_in_even.shape[2] - MLA_ROPE
    kr_e = w_in_even[:, :, kr0::2]
    kr_o = w_in_even[:, :, kr0 + 1::2]
    z_lo = jnp.zeros(w_in_even.shape[:2] + (MLA_NOPE,), F32)
    z_hi = jnp.zeros(w_in_even.shape[:2] + (HEAD_PAD - MLA_NOPE - MLA_ROPE,), F32)
    w_in_e = jnp.concatenate([w_in_even[:, :, :kr0], z_lo, kr_e, kr_o, z_hi, z_lo, kr_o, kr_e, z_hi],
                             axis=-1).astype(BF16)
    qk_w = 2 * RET_HEADS * HEAD_DIM
    perm = (np.arange(qk_w) // HEAD_DIM) * HEAD_DIM + np.tile(_split_pairs_perm(HEAD_DIM),
                                                               qk_w // HEAD_DIM)
    w_in_o = jnp.concatenate([w_in_odd[:, :, perm], w_in_odd[:, :, qk_w:]], axis=-1).astype(BF16)

    s5_mats = jax.vmap(_s5_matrices)(s5_a_re, s5_a_im, s5_log_dt, s5_b_re, s5_b_im, s5_c_re, s5_c_im)
    mla_w = jax.vmap(_mla_weights)(mla_w_uq, mla_w_ukv)

    w1 = ffn_w1.astype(BF16)
    w3 = ffn_w3.astype(BF16)
    w2 = ffn_w2.astype(BF16)
    w_out_e = w_out_even.astype(BF16)
    w_out_o = w_out_odd.astype(BF16)
    w_glu = s5_w_glu.astype(BF16)

    for li in range(depth):
        j = li // 2
        final = li == depth - 1
        if li % 2 == 0:
            proj = _in_proj(xa, mods[li], norm1_g[li], w_in_e[j], ret_cos, ret_sin,
                            lc=lc, tn=w_in_e.shape[2], n_rope=0, out_dtype=F32)
            y_s5 = _s5_mixer(proj, tuple(m[j] for m in s5_mats), lc=lc)
            attn = _mla_mixer(proj, mla_q_norm[j], mla_kv_norm[j],
                              tuple(w[j] for w in mla_w), mla_tabs, lc=lc)
            xa = _out_ffn(xa, y_s5, attn, mods[li], norm2_g[li], w_out_e[j], w1[li], w3[li], w2[li],
                          final_norm, lc=lc, final=final, s5_u=proj, s5_d=s5_d[j], w_glu=w_glu[j])
        else:
            proj = _in_proj(xa, mods[li], norm1_g[li], w_in_o[j], ret_cos, ret_sin,
                            lc=lc, tn=RET_HEADS * HEAD_DIM, n_rope=2, out_dtype=BF16)
            ret = _retention_mixer(proj, ret_gn[j], ret_consts, lc=lc)
            hg = _hgrn2_mixer(proj, hg_lb[j], hg_gn[j], hg_consts, lc=lc, col0=4 * RET_HEADS)
            xa = _out_ffn(xa, ret, hg, mods[li], norm2_g[li], w_out_o[j], w1[li], w3[li], w2[li],
                          final_norm, lc=lc, final=final)
    return xa[:, lc:, :]
```
